```python
import jax, jax.numpy as jnp
from jax import lax
import numpy as np

D_MODEL = 2048
BATCH = 2
SEQ = 4096
DEPTH = 2

N_MIXERS = 2
N_EVEN = (DEPTH + 1) // 2
N_ODD = DEPTH // 2
PLE_DIM = 256
BLOCK = 128
HEAD_DIM = 128
N_HEADS = D_MODEL // HEAD_DIM
GM_WIDTH = D_MODEL
GM_CHUNK = 128
GM_GROUPS = 16
GM_GROUP_DIM = GM_WIDTH // GM_GROUPS
D_FF = 5632
N_EXPERTS = 8
TOP_K = 2
D_FF_EXPERT = 7168
LN_EPS = 1e-5
ALPHA = (2.0 * DEPTH) ** 0.25
BETA = (8.0 * DEPTH) ** -0.25

kernel_name = "fox_gmlp_moe_deepnorm_hybrid"


def layer_norm(x, g, b):
    xf = x.astype(jnp.float32)
    mu = jnp.mean(xf, axis=-1, keepdims=True)
    var = jnp.mean(jnp.square(xf - mu), axis=-1, keepdims=True)
    y = (xf - mu) * lax.rsqrt(var + LN_EPS) * g.astype(jnp.float32) + b.astype(jnp.float32)
    return y.astype(x.dtype)


def fox_mixer(x, w_in, b_f, w_o):
    B, S, _ = x.shape
    h = x @ w_in
    q, k, v, f = jnp.split(h, [D_MODEL, 2 * D_MODEL, 3 * D_MODEL], axis=-1)
    q = q.reshape(B, S, N_HEADS, HEAD_DIM)
    k = k.reshape(B, S, N_HEADS, HEAD_DIM)
    v = v.reshape(B, S, N_HEADS, HEAD_DIM)
    logf = jax.nn.log_sigmoid((f + b_f).astype(jnp.float32))
    c = jnp.cumsum(logf, axis=1).transpose(0, 2, 1)
    nb = S // BLOCK
    qb = q.reshape(B, nb, BLOCK, N_HEADS, HEAD_DIM).swapaxes(0, 1)
    cb = c.reshape(B, N_HEADS, nb, BLOCK).transpose(2, 0, 1, 3)
    kpos = jnp.arange(S)
    scale = HEAD_DIM ** -0.5

    def one_block(args):
        i, qi, ci = args
        s = jnp.einsum('bqhd,bkhd->bhqk', qi, k).astype(jnp.float32) * scale
        s = s + (ci[..., :, None] - c[:, :, None, :])
        qpos = i * BLOCK + jnp.arange(BLOCK)
        s = jnp.where(kpos[None, :] <= qpos[:, None], s, -jnp.inf)
        pr = jax.nn.softmax(s, axis=-1).astype(v.dtype)
        return jnp.einsum('bhqk,bkhd->bqhd', pr, v)

    o = lax.map(one_block, (jnp.arange(nb), qb, cb))
    o = o.swapaxes(0, 1).reshape(B, S, D_MODEL)
    return o @ w_o


def gmlp_mixer(x, w_in, ln_v_g, ln_v_b, w_s, b_s, w_o):
    B, S, _ = x.shape
    z = jax.nn.gelu(x @ w_in)
    u, v = jnp.split(z, 2, axis=-1)
    v = layer_norm(v, ln_v_g, ln_v_b)
    nc = S // GM_CHUNK
    v = v.reshape(B, nc, GM_CHUNK, GM_GROUPS, GM_GROUP_DIM)
    causal = jnp.tril(jnp.ones((GM_CHUNK, GM_CHUNK), dtype=bool))
    w = jnp.where(causal[None], w_s, 0.0)
    mixed = jnp.einsum('gts,bcsgd->bctgd', w, v) + b_s.T[:, :, None]
    y = u * mixed.reshape(B, S, GM_WIDTH)
    return y @ w_o


def swiglu(x, w_gu, w_down):
    g, u = jnp.split(x @ w_gu, 2, axis=-1)
    return (jax.nn.silu(g) * u) @ w_down


def moe_swiglu(x, w_router, w_gu, w_down):
    B, S, Dm = x.shape
    t = x.reshape(B * S, Dm)
    logits = (t @ w_router).astype(jnp.float32)
    top_val, top_idx = lax.top_k(logits, TOP_K)
    gates = jax.nn.softmax(top_val, axis=-1)
    combine = jnp.sum(jax.nn.one_hot(top_idx, N_EXPERTS, dtype=jnp.float32)
                      * gates[..., None], axis=1)
    y = jnp.zeros_like(t)
    for e in range(N_EXPERTS):
        y = y + combine[:, e:e + 1].astype(t.dtype) * swiglu(t, w_gu[e], w_down[e])
    return y.reshape(B, S, Dm)


def setup_inputs(seed: int = 0) -> dict:
    key = jax.random.key(seed)
    ks = jax.random.split(key, 24)
    D, H, W = D_MODEL, N_HEADS, GM_WIDTH
    nrm = jax.random.normal
    x = nrm(ks[0], (BATCH, SEQ, D), jnp.float32)
    p = nrm(ks[1], (DEPTH, BATCH, SEQ, PLE_DIM), jnp.float32)
    col_scale = jnp.concatenate([jnp.ones((2 * D,)), jnp.full((D,), BETA), jnp.ones((H,))])
    fox_w_in = nrm(ks[2], (N_EVEN, D, 3 * D + H)) * D ** -0.5 * col_scale
    fox_b_f = jax.random.uniform(ks[3], (N_EVEN, H), minval=1.0, maxval=4.0)
    fox_w_o = nrm(ks[4], (N_EVEN, D, D)) * D ** -0.5 * BETA
    gm_w_in = nrm(ks[5], (N_ODD, D, 2 * W)) * D ** -0.5
    gm_ln_v_g = 1.0 + 0.05 * nrm(ks[6], (N_ODD, W))
    gm_ln_v_b = 0.01 * nrm(ks[7], (N_ODD, W))
    gm_w_s = nrm(ks[8], (N_ODD, GM_GROUPS, GM_CHUNK, GM_CHUNK)) * GM_CHUNK ** -0.5
    gm_b_s = 1.0 + 0.1 * nrm(ks[9], (N_ODD, GM_GROUPS, GM_CHUNK))
    gm_w_o = nrm(ks[10], (N_ODD, W, D)) * W ** -0.5 * BETA
    ffn_w_gu = nrm(ks[11], (N_EVEN, D, 2 * D_FF)) * D ** -0.5
    ffn_w_down = nrm(ks[12], (N_EVEN, D_FF, D)) * D_FF ** -0.5 * BETA
    moe_w_router = nrm(ks[13], (N_ODD, D, N_EXPERTS)) * D ** -0.5
    moe_w_gu = nrm(ks[14], (N_ODD, N_EXPERTS, D, 2 * D_FF_EXPERT)) * D ** -0.5
    moe_w_down = nrm(ks[15], (N_ODD, N_EXPERTS, D_FF_EXPERT, D)) * D_FF_EXPERT ** -0.5 * BETA
    ln_mix_g = 1.0 + 0.05 * nrm(ks[16], (DEPTH, D))
    ln_mix_b = 0.01 * nrm(ks[17], (DEPTH, D))
    ln_ch_g = 1.0 + 0.05 * nrm(ks[18], (DEPTH, D))
    ln_ch_b = 0.01 * nrm(ks[19], (DEPTH, D))
    ple_w_proj = nrm(ks[20], (DEPTH, PLE_DIM, D)) * PLE_DIM ** -0.5
    ple_w_gate = nrm(ks[21], (DEPTH, D, D)) * D ** -0.5
    return {"x": x, "p": p,
            "fox_w_in": fox_w_in, "fox_b_f": fox_b_f, "fox_w_o": fox_w_o,
            "gm_w_in": gm_w_in, "gm_ln_v_g": gm_ln_v_g, "gm_ln_v_b": gm_ln_v_b,
            "gm_w_s": gm_w_s, "gm_b_s": gm_b_s, "gm_w_o": gm_w_o,
            "ffn_w_gu": ffn_w_gu, "ffn_w_down": ffn_w_down,
            "moe_w_router": moe_w_router, "moe_w_gu": moe_w_gu, "moe_w_down": moe_w_down,
            "ln_mix_g": ln_mix_g, "ln_mix_b": ln_mix_b, "ln_ch_g": ln_ch_g, "ln_ch_b": ln_ch_b,
            "ple_w_proj": ple_w_proj, "ple_w_gate": ple_w_gate}


def reference(x, p, fox_w_in, fox_b_f, fox_w_o, gm_w_in, gm_ln_v_g, gm_ln_v_b,
              gm_w_s, gm_b_s, gm_w_o, ffn_w_gu, ffn_w_down, moe_w_router, moe_w_gu,
              moe_w_down, ln_mix_g, ln_mix_b, ln_ch_g, ln_ch_b, ple_w_proj, ple_w_gate):
    for i in range(DEPTH):
        j = i // N_MIXERS
        if i % N_MIXERS == 0:
            mix = fox_mixer(x, fox_w_in[j], fox_b_f[j], fox_w_o[j])
        else:
            mix = gmlp_mixer(x, gm_w_in[j], gm_ln_v_g[j], gm_ln_v_b[j],
                             gm_w_s[j], gm_b_s[j], gm_w_o[j])
        x = layer_norm(ALPHA * x + mix, ln_mix_g[i], ln_mix_b[i])
        if i % 2 == 0:
            ch = swiglu(x, ffn_w_gu[i // 2], ffn_w_down[i // 2])
        else:
            ch = moe_swiglu(x, moe_w_router[i // 2], moe_w_gu[i // 2], moe_w_down[i // 2])
        x = layer_norm(ALPHA * x + ch, ln_ch_g[i], ln_ch_b[i])
        x = x + jax.nn.sigmoid(x @ ple_w_gate[i]) * (p[i] @ ple_w_proj[i])
    return x
```

```python
import functools

import jax
import jax.numpy as jnp
from jax import lax
from jax.experimental import pallas as pl
from jax.experimental.pallas import tpu as pltpu

F32 = jnp.float32
BF16 = jnp.bfloat16

LN_EPS = 1e-5
DEPTH = 2
ALPHA = (2.0 * DEPTH) ** 0.25
HEAD_DIM = 128
GM_CHUNK = 128
GM_GROUPS = 16
TOP_K = 2
NEG = -1e30
V7X_VMEM_LIMIT_BYTES = 56 * 2**20
EXPERT_ROWS = 16
MOE_TM = 256


def _params(n_axes):
    return pltpu.CompilerParams(dimension_semantics=("arbitrary",) * n_axes,
                                vmem_limit_bytes=V7X_VMEM_LIMIT_BYTES)


def _dot(a, b):
    return jnp.dot(a, b, preferred_element_type=F32)


def _dot_nt(a, b):
    return lax.dot_general(a, b, (((1,), (1,)), ((), ())), preferred_element_type=F32)


def _sigmoid(x):
    return 1.0 / (1.0 + jnp.exp(-x))


def _gelu_tanh(x):
    return 0.5 * x * (1.0 + jnp.tanh(0.7978845608028654 * (x + 0.044715 * (x * x * x))))


def _layer_norm(y, g, b):
    mu = jnp.mean(y, axis=-1, keepdims=True)
    d = y - mu
    var = jnp.mean(d * d, axis=-1, keepdims=True)
    return d * lax.rsqrt(var + LN_EPS) * g + b


def _resident(block_shape, index_map):
    return pl.BlockSpec(block_shape, index_map, pipeline_mode=pl.Buffered(1))


def _mm_kernel(a_ref, w_ref, o_ref, *, act):
    y = _dot(a_ref[...], w_ref[...])
    if act == "gelu":
        y = _gelu_tanh(y)
    o_ref[...] = y.astype(o_ref.dtype)


def _mm(a, w, *, tm, tn, out_dtype, act=None):
    M, K = a.shape
    N = w.shape[1]
    tm, tn = min(tm, M), min(tn, N)
    return pl.pallas_call(
        functools.partial(_mm_kernel, act=act),
        grid=(N // tn, M // tm),
        in_specs=[pl.BlockSpec((tm, K), lambda n, m: (m, 0)),
                  pl.BlockSpec((K, tn), lambda n, m: (0, n))],
        out_specs=pl.BlockSpec((tm, tn), lambda n, m: (m, n)),
        out_shape=jax.ShapeDtypeStruct((M, N), out_dtype),
        compiler_params=_params(2),
        name="mm_" + (act or "plain"),
    )(a, w)


def _mm_res_ln_kernel(a_ref, w_ref, r_ref, g_ref, b_ref, of_ref, ob_ref):
    y = ALPHA * r_ref[...] + _dot(a_ref[...], w_ref[...])
    y = _layer_norm(y, g_ref[...], b_ref[...])
    of_ref[...] = y
    ob_ref[...] = y.astype(BF16)


def _mm_res_ln(a, w, res, g, b, *, tm):
    M, K = a.shape
    D = w.shape[1]
    tm = min(tm, M)
    return pl.pallas_call(
        _mm_res_ln_kernel,
        grid=(M // tm,),
        in_specs=[pl.BlockSpec((tm, K), lambda m: (m, 0)),
                  _resident((K, D), lambda m: (0, 0)),
                  pl.BlockSpec((tm, D), lambda m: (m, 0)),
                  _resident((1, D), lambda m: (0, 0)),
                  _resident((1, D), lambda m: (0, 0))],
        out_specs=[pl.BlockSpec((tm, D), lambda m: (m, 0)),
                   pl.BlockSpec((tm, D), lambda m: (m, 0))],
        out_shape=[jax.ShapeDtypeStruct((M, D), F32), jax.ShapeDtypeStruct((M, D), BF16)],
        compiler_params=_params(1),
        name="mm_res_ln",
    )(a, w, res, g.reshape(1, D), b.reshape(1, D))


def _swiglu_up_kernel(x_ref, wg_ref, wu_ref, o_ref):
    x = x_ref[...]
    g = _dot(x, wg_ref[...])
    u = _dot(x, wu_ref[...])
    o_ref[...] = (g * _sigmoid(g) * u).astype(o_ref.dtype)


def _swiglu_up(x, w_gu, *, tm, tn):
    M, K = x.shape
    F = w_gu.shape[1] // 2
    tm, tn = min(tm, M), min(tn, F)
    nf = F // tn
    return pl.pallas_call(
        _swiglu_up_kernel,
        grid=(nf, M // tm),
        in_specs=[pl.BlockSpec((tm, K), lambda n, m: (m, 0)),
                  pl.BlockSpec((K, tn), lambda n, m: (0, n)),
                  pl.BlockSpec((K, tn), lambda n, m: (0, nf + n))],
        out_specs=pl.BlockSpec((tm, tn), lambda n, m: (m, n)),
        out_shape=jax.ShapeDtypeStruct((M, F), BF16),
        compiler_params=_params(2),
        name="swiglu_up",
    )(x, w_gu, w_gu)


def _ple_kernel(xb_ref, wg_ref, p_ref, wp_ref, xf_ref, of_ref, ob_ref):
    gate = _sigmoid(_dot(xb_ref[...], wg_ref[...]))
    y = xf_ref[...] + gate * _dot(p_ref[...], wp_ref[...])
    of_ref[...] = y
    ob_ref[...] = y.astype(BF16)


def _ple(xb, xf, w_gate, p, w_proj, *, tm, tn):
    M, D = xf.shape
    P = p.shape[1]
    tm, tn = min(tm, M), min(tn, D)
    return pl.pallas_call(
        _ple_kernel,
        grid=(D // tn, M // tm),
        in_specs=[pl.BlockSpec((tm, D), lambda n, m: (m, 0)),
                  pl.BlockSpec((D, tn), lambda n, m: (0, n)),
                  pl.BlockSpec((tm, P), lambda n, m: (m, 0)),
                  pl.BlockSpec((P, tn), lambda n, m: (0, n)),
                  pl.BlockSpec((tm, tn), lambda n, m: (m, n))],
        out_specs=[pl.BlockSpec((tm, tn), lambda n, m: (m, n)),
                   pl.BlockSpec((tm, tn), lambda n, m: (m, n))],
        out_shape=[jax.ShapeDtypeStruct((M, D), F32), jax.ShapeDtypeStruct((M, D), BF16)],
        compiler_params=_params(2),
        name="ple",
    )(xb, w_gate, p, w_proj, xf)


def _split3_bf16(x):
    hi = x.astype(BF16)
    r = x - hi.astype(F32)
    mid = r.astype(BF16)
    lo = (r - mid.astype(F32)).astype(BF16)
    return hi, mid, lo


def _fox_gate_kernel(x_ref, wft_ref, bf_ref, c_ref, carry_ref, *, ts):
    @pl.when(pl.program_id(1) == 0)
    def _():
        carry_ref[...] = jnp.zeros_like(carry_ref)

    f = _dot_nt(wft_ref[...], x_ref[...]) + bf_ref[...]
    logf = jnp.minimum(f, 0.0) - jnp.log1p(jnp.exp(-jnp.abs(f)))
    row = lax.broadcasted_iota(jnp.int32, (ts, ts), 0)
    col = lax.broadcasted_iota(jnp.int32, (ts, ts), 1)
    tri = jnp.where(row <= col, 1.0, 0.0).astype(BF16)
    hi, mid, lo = _split3_bf16(logf)
    c = _dot(hi, tri) + _dot(mid, tri) + _dot(lo, tri) + carry_ref[...]
    c_ref[0] = c
    carry_ref[...] = c[:, ts - 1:ts]


def _fox_gate(xb, wft, b_f, *, batch, ts):
    T, D = xb.shape
    H = wft.shape[0]
    S = T // batch
    ts = min(ts, S)
    ns = S // ts
    return pl.pallas_call(
        functools.partial(_fox_gate_kernel, ts=ts),
        grid=(batch, ns),
        in_specs=[pl.BlockSpec((ts, D), lambda b, i: (b * ns + i, 0)),
                  _resident((H, D), lambda b, i: (0, 0)),
                  _resident((H, 1), lambda b, i: (0, 0))],
        out_specs=pl.BlockSpec((1, H, ts), lambda b, i: (b, 0, i)),
        out_shape=jax.ShapeDtypeStruct((batch, H, S), F32),
        scratch_shapes=[pltpu.VMEM((H, 1), F32)],
        compiler_params=_params(2),
        name="fox_gate",
    )(xb, wft, b_f.reshape(H, 1).astype(F32))


def _fox_attn_kernel(q_ref, k_ref, v_ref, c_ref, o_ref, *, tq, scale):
    qi = pl.program_id(2)
    q = (q_ref[...].astype(F32) * scale).astype(BF16)

    def block(j, carry, masked):
        m, l, acc = carry
        start = pl.multiple_of(j * tq, tq)
        k = k_ref[pl.ds(start, tq), :]
        v = v_ref[pl.ds(start, tq), :]
        s = _dot_nt(q, k) - c_ref[0, :, pl.ds(start, tq)]
        if masked:
            row = lax.broadcasted_iota(jnp.int32, (tq, tq), 0)
            col = lax.broadcasted_iota(jnp.int32, (tq, tq), 1)
            s = jnp.where(row >= col, s, NEG)
        m_new = jnp.maximum(m, jnp.max(s, axis=-1, keepdims=True))
        p = jnp.exp(s - m_new)
        a = jnp.exp(m - m_new)
        l = a * l + jnp.sum(p, axis=-1, keepdims=True)
        acc = a * acc + _dot(p.astype(BF16), v)
        return m_new, l, acc

    init = (jnp.full((tq, 1), NEG, F32), jnp.zeros((tq, 1), F32), jnp.zeros((tq, HEAD_DIM), F32))
    carry = lax.fori_loop(0, qi, lambda j, c: block(j, c, False), init)
    _, l, acc = block(qi, carry, True)
    o_ref[...] = (acc / l).astype(o_ref.dtype)


def _fox_attn(qkv, c, *, batch, tq):
    T, D3 = qkv.shape
    D = D3 // 3
    H = D // HEAD_DIM
    S = T // batch
    tq = min(tq, S)
    nq = S // tq
    c3 = c.reshape(batch * H, 1, S)
    return pl.pallas_call(
        functools.partial(_fox_attn_kernel, tq=tq, scale=HEAD_DIM ** -0.5),
        grid=(batch, H, nq),
        in_specs=[pl.BlockSpec((tq, HEAD_DIM), lambda b, h, i: (b * nq + i, h)),
                  pl.BlockSpec((S, HEAD_DIM), lambda b, h, i: (b, H + h)),
                  pl.BlockSpec((S, HEAD_DIM), lambda b, h, i: (b, 2 * H + h)),
                  pl.BlockSpec((1, 1, S), lambda b, h, i: (b * H + h, 0, 0))],
        out_specs=pl.BlockSpec((tq, HEAD_DIM), lambda b, h, i: (b * nq + i, h)),
        out_shape=jax.ShapeDtypeStruct((T, D), BF16),
        compiler_params=_params(3),
        name="fox_attn",
    )(qkv, qkv, qkv, c3)


def _gm_spatial_kernel(u_ref, v_ref, g_ref, b_ref, ws_ref, bs_ref, o_ref, *, n_chunks):
    C = GM_CHUNK
    vn = _layer_norm(v_ref[...].astype(F32), g_ref[...], b_ref[...]).astype(BF16)
    row = lax.broadcasted_iota(jnp.int32, (C, C), 0)
    col = lax.broadcasted_iota(jnp.int32, (C, C), 1)
    for grp in range(ws_ref.shape[0]):
        w = jnp.where(row >= col, ws_ref[grp], 0.0).astype(BF16)
        cols = slice(grp * C, (grp + 1) * C)
        for ch in range(n_chunks):
            rows = slice(ch * C, (ch + 1) * C)
            mixed = _dot(w, vn[rows, cols]) + bs_ref[:, cols]
            o_ref[rows, cols] = (u_ref[rows, cols].astype(F32) * mixed).astype(o_ref.dtype)


def _gm_spatial(z, ln_g, ln_b, w_s, bias_full, *, tm):
    T, W2 = z.shape
    W = W2 // 2
    G, C, _ = w_s.shape
    tm = min(tm, T)
    return pl.pallas_call(
        functools.partial(_gm_spatial_kernel, n_chunks=tm // C),
        grid=(T // tm,),
        in_specs=[pl.BlockSpec((tm, W), lambda i: (i, 0)),
                  pl.BlockSpec((tm, W), lambda i: (i, 1)),
                  _resident((1, W), lambda i: (0, 0)),
                  _resident((1, W), lambda i: (0, 0)),
                  _resident((G, C, C), lambda i: (0, 0, 0)),
                  _resident((C, W), lambda i: (0, 0))],
        out_specs=pl.BlockSpec((tm, W), lambda i: (i, 0)),
        out_shape=jax.ShapeDtypeStruct((T, W), BF16),
        compiler_params=_params(1),
        name="gm_spatial",
    )(z, z, ln_g.reshape(1, W), ln_b.reshape(1, W), w_s, bias_full)


def _router_kernel(x_ref, wrt_ref, o_ref, cnt_ref, carry_ref, *, tm, n_exp):
    @pl.when(pl.program_id(0) == 0)
    def _():
        carry_ref[...] = jnp.zeros_like(carry_ref)

    x = x_ref[...]
    xh = x.astype(BF16)
    xl = (x - xh.astype(F32)).astype(BF16)
    wh, wl = wrt_ref[0], wrt_ref[1]
    logits = _dot_nt(wh, xh) + _dot_nt(wh, xl) + _dot_nt(wl, xh)
    row = lax.broadcasted_iota(jnp.int32, logits.shape, 0)
    logits = jnp.where(row < n_exp, logits, NEG)
    top1 = jnp.max(logits, axis=0, keepdims=True)
    idx1 = jnp.min(jnp.where(logits == top1, row, EXPERT_ROWS), axis=0, keepdims=True)
    rest = jnp.where(row == idx1, NEG, logits)
    top2 = jnp.max(rest, axis=0, keepdims=True)
    idx2 = jnp.min(jnp.where(rest == top2, row, EXPERT_ROWS), axis=0, keepdims=True)
    e = jnp.exp(top2 - top1)
    gate1 = 1.0 / (1.0 + e)
    gate2 = e / (1.0 + e)

    sel = jnp.where((row == idx1) | (row == idx2), 1.0, 0.0)
    r = lax.broadcasted_iota(jnp.int32, (tm, tm), 0)
    c = lax.broadcasted_iota(jnp.int32, (tm, tm), 1)
    tri = jnp.where(r <= c, 1.0, 0.0).astype(BF16)
    incl = _dot(sel.astype(BF16), tri)
    excl = incl - sel + carry_ref[...]
    rank1 = jnp.sum(jnp.where(row == idx1, excl, 0.0), axis=0, keepdims=True)
    rank2 = jnp.sum(jnp.where(row == idx2, excl, 0.0), axis=0, keepdims=True)
    total = carry_ref[...] + incl[:, tm - 1:tm]
    carry_ref[...] = total
    cnt_ref[...] = jnp.broadcast_to(total, cnt_ref.shape)
    o_ref[...] = jnp.concatenate(
        [idx1.astype(F32), idx2.astype(F32), gate1, gate2, rank1, rank2,
         jnp.zeros((2, tm), F32)], axis=0)


def _router(xf, w_router, *, tm):
    T, D = xf.shape
    n_exp = w_router.shape[1]
    tm = min(tm, T)
    wt = jnp.zeros((EXPERT_ROWS, D), F32).at[:n_exp].set(w_router.T.astype(F32))
    wh = wt.astype(BF16)
    wl = (wt - wh.astype(F32)).astype(BF16)
    return pl.pallas_call(
        functools.partial(_router_kernel, tm=tm, n_exp=n_exp),
        grid=(T // tm,),
        in_specs=[pl.BlockSpec((tm, D), lambda i: (i, 0)),
                  _resident((2, EXPERT_ROWS, D), lambda i: (0, 0, 0))],
        out_specs=[pl.BlockSpec((8, tm), lambda i: (0, i)),
                   pl.BlockSpec((EXPERT_ROWS, 128), lambda i: (0, 0))],
        out_shape=[jax.ShapeDtypeStruct((8, T), F32),
                   jax.ShapeDtypeStruct((EXPERT_ROWS, 128), F32)],
        scratch_shapes=[pltpu.VMEM((EXPERT_ROWS, 1), F32)],
        compiler_params=_params(1),
        name="router",
    )(xf, jnp.stack([wh, wl]))


def _moe_up_kernel(te_ref, nu_ref, x_ref, wg_ref, wu_ref, o_ref):
    @pl.when(pl.program_id(1) < nu_ref[0])
    def _():
        x = x_ref[...]
        g = _dot(x, wg_ref[...])
        u = _dot(x, wu_ref[...])
        o_ref[...] = (g * _sigmoid(g) * u).astype(o_ref.dtype)


def _moe_up(xs, w_gu, tile_expert, n_used, *, tm, tn):
    P, K = xs.shape
    F = w_gu.shape[2] // 2
    tn = min(tn, F)
    nf = F // tn

    def row(n, m, te, nu):
        return jnp.minimum(m, nu[0] - 1)

    return pl.pallas_call(
        _moe_up_kernel,
        grid_spec=pltpu.PrefetchScalarGridSpec(
            num_scalar_prefetch=2,
            grid=(nf, P // tm),
            in_specs=[pl.BlockSpec((tm, K), lambda n, m, te, nu: (row(n, m, te, nu), 0)),
                      pl.BlockSpec((None, K, tn), lambda n, m, te, nu: (te[m], 0, n)),
                      pl.BlockSpec((None, K, tn), lambda n, m, te, nu: (te[m], 0, nf + n))],
            out_specs=pl.BlockSpec((tm, tn), lambda n, m, te, nu: (row(n, m, te, nu), n))),
        out_shape=jax.ShapeDtypeStruct((P, F), BF16),
        compiler_params=_params(2),
        name="moe_up",
    )(tile_expert, n_used, xs, w_gu, w_gu)


def _moe_down_kernel(te_ref, nu_ref, h_ref, w_ref, o_ref):
    @pl.when(pl.program_id(1) < nu_ref[0])
    def _():
        o_ref[...] = _dot(h_ref[...], w_ref[...]).astype(o_ref.dtype)


def _moe_down(hs, w_down, tile_expert, n_used, *, tm, tn):
    P, F = hs.shape
    D = w_down.shape[2]
    tn = min(tn, D)

    def row(n, m, te, nu):
        return jnp.minimum(m, nu[0] - 1)

    return pl.pallas_call(
        _moe_down_kernel,
        grid_spec=pltpu.PrefetchScalarGridSpec(
            num_scalar_prefetch=2,
            grid=(D // tn, P // tm),
            in_specs=[pl.BlockSpec((tm, F), lambda n, m, te, nu: (row(n, m, te, nu), 0)),
                      pl.BlockSpec((None, F, tn), lambda n, m, te, nu: (te[m], 0, n))],
            out_specs=pl.BlockSpec((tm, tn), lambda n, m, te, nu: (row(n, m, te, nu), n))),
        out_shape=jax.ShapeDtypeStruct((P, D), BF16),
        compiler_params=_params(2),
        name="moe_down",
    )(tile_expert, n_used, hs, w_down)


def _combine_ln_kernel(r_ref, o1_ref, o2_ref, g1_ref, g2_ref, g_ref, b_ref, of_ref, ob_ref):
    ch = g1_ref[...] * o1_ref[...].astype(F32) + g2_ref[...] * o2_ref[...].astype(F32)
    y = _layer_norm(ALPHA * r_ref[...] + ch, g_ref[...], b_ref[...])
    of_ref[...] = y
    ob_ref[...] = y.astype(BF16)


def _combine_ln(res, o1, o2, g1, g2, g, b, *, tm):
    M, D = res.shape
    tm = min(tm, M)
    tile = pl.BlockSpec((tm, D), lambda m: (m, 0))
    col = pl.BlockSpec((tm, 1), lambda m: (m, 0))
    vec = _resident((1, D), lambda m: (0, 0))
    return pl.pallas_call(
        _combine_ln_kernel,
        grid=(M // tm,),
        in_specs=[tile, tile, tile, col, col, vec, vec],
        out_specs=[tile, tile],
        out_shape=[jax.ShapeDtypeStruct((M, D), F32), jax.ShapeDtypeStruct((M, D), BF16)],
        compiler_params=_params(1),
        name="combine_ln",
    )(res, o1, o2, g1, g2, g.reshape(1, D), b.reshape(1, D))


def _moe_layer(xf, xb, w_router, w_gu, w_down, ln_g, ln_b):
    T, D = xf.shape
    n_exp = w_router.shape[1]
    tm = min(MOE_TM, T)
    n_tiles = (T * TOP_K) // tm + n_exp

    route, counts = _router(xf, w_router, tm=512)
    idx = route[0:2].astype(jnp.int32)
    gates = route[2:4]
    rank = route[4:6].astype(jnp.int32)
    counts = counts[:n_exp, 0].astype(jnp.int32)
    tiles_per_expert = (counts + tm - 1) // tm
    tile_end = jnp.cumsum(tiles_per_expert)
    n_used = tile_end[-1]
    row_start = (tile_end - tiles_per_expert) * tm
    pos = row_start[idx] + rank
    tile_ids = jnp.arange(n_tiles, dtype=jnp.int32)
    tile_expert = jnp.sum(tile_ids[:, None] >= tile_end[None, :], axis=1).astype(jnp.int32)
    last_expert = jnp.max(jnp.where(tiles_per_expert > 0, jnp.arange(n_exp, dtype=jnp.int32), 0))
    tile_expert = jnp.minimum(tile_expert, last_expert)
    n_used = n_used.reshape(1).astype(jnp.int32)

    tokens = jnp.tile(jnp.arange(T, dtype=jnp.int32), TOP_K)
    slot_token = jnp.zeros((n_tiles * tm,), jnp.int32).at[pos.reshape(-1)].set(tokens)
    xs = jnp.take(xb, slot_token, axis=0)

    hs = _moe_up(xs, w_gu, tile_expert, n_used, tm=tm, tn=1792)
    ys = _moe_down(hs, w_down, tile_expert, n_used, tm=tm, tn=1024)
    o1 = jnp.take(ys, pos[0], axis=0)
    o2 = jnp.take(ys, pos[1], axis=0)
    return _combine_ln(xf, o1, o2, gates[0].reshape(T, 1), gates[1].reshape(T, 1), ln_g, ln_b, tm=512)


def kernel(x, p, fox_w_in, fox_b_f, fox_w_o, gm_w_in, gm_ln_v_g, gm_ln_v_b, gm_w_s, gm_b_s, gm_w_o, ffn_w_gu, ffn_w_down, moe_w_router, moe_w_gu, moe_w_down, ln_mix_g, ln_mix_b, ln_ch_g, ln_ch_b, ple_w_proj, ple_w_gate):
    B, S, D = x.shape
    T = B * S
    H = D // HEAD_DIM
    xf = x.reshape(T, D)
    xb = xf.astype(BF16)
    pb = p.reshape(p.shape[0], T, p.shape[-1]).astype(BF16)

    w_in = fox_w_in[0]
    qkv = _mm(xb, w_in[:, :3 * D].astype(BF16), tm=1024, tn=1024, out_dtype=BF16)
    c = _fox_gate(xb, w_in[:, 3 * D:].T.astype(BF16), fox_b_f[0], batch=B, ts=512)
    attn = _fox_attn(qkv, c, batch=B, tq=512)
    xf, xb = _mm_res_ln(attn, fox_w_o[0].astype(BF16), xf, ln_mix_g[0], ln_mix_b[0], tm=256)
    hid = _swiglu_up(xb, ffn_w_gu[0].astype(BF16), tm=1024, tn=512)
    xf, xb = _mm_res_ln(hid, ffn_w_down[0].astype(BF16), xf, ln_ch_g[0], ln_ch_b[0], tm=256)
    xf, xb = _ple(xb, xf, ple_w_gate[0].astype(BF16), pb[0], ple_w_proj[0].astype(BF16), tm=512, tn=1024)

    z = _mm(xb, gm_w_in[0].astype(BF16), tm=1024, tn=1024, out_dtype=BF16, act="gelu")
    bias_full = jnp.repeat(gm_b_s[0].T.astype(F32), D // gm_b_s.shape[1], axis=1)
    y = _gm_spatial(z, gm_ln_v_g[0], gm_ln_v_b[0], gm_w_s[0], bias_full, tm=512)
    xf, xb = _mm_res_ln(y, gm_w_o[0].astype(BF16), xf, ln_mix_g[1], ln_mix_b[1], tm=256)
    xf, xb = _moe_layer(xf, xb, moe_w_router[0], moe_w_gu[0].astype(BF16), moe_w_down[0].astype(BF16),
                        ln_ch_g[1], ln_ch_b[1])
    xf, _ = _ple(xb, xf, ple_w_gate[1].astype(BF16), pb[1], ple_w_proj[1].astype(BF16), tm=512, tn=1024)
    return xf.reshape(B, S, D)
```

```python
import functools

import jax
import jax.numpy as jnp
from jax import lax
from jax.experimental import pallas as pl
from jax.experimental.pallas import tpu as pltpu

F32 = jnp.float32
BF16 = jnp.bfloat16

LN_EPS = 1e-5
DEPTH = 2
ALPHA = (2.0 * DEPTH) ** 0.25
HEAD_DIM = 128
GM_CHUNK = 128
GM_GROUPS = 16
TOP_K = 2
NEG = -1e30
V7X_VMEM_LIMIT_BYTES = 56 * 2**20
EXPERT_ROWS = 16
MOE_TM = 256


def _params(n_axes):
    return pltpu.CompilerParams(dimension_semantics=("arbitrary",) * n_axes,
                                vmem_limit_bytes=V7X_VMEM_LIMIT_BYTES)


def _dot(a, b):
    return jnp.dot(a, b, preferred_element_type=F32)


def _dot_nt(a, b):
    return lax.dot_general(a, b, (((1,), (1,)), ((), ())), preferred_element_type=F32)


def _sigmoid(x):
    return 1.0 / (1.0 + jnp.exp(-x))


def _gelu_tanh(x):
    return 0.5 * x * (1.0 + jnp.tanh(0.7978845608028654 * (x + 0.044715 * (x * x * x))))


def _layer_norm(y, g, b):
    mu = jnp.mean(y, axis=-1, keepdims=True)
    d = y - mu
    var = jnp.mean(d * d, axis=-1, keepdims=True)
    return d * lax.rsqrt(var + LN_EPS) * g + b


def _resident(block_shape, index_map):
    return pl.BlockSpec(block_shape, index_map, pipeline_mode=pl.Buffered(1))


def _mm_kernel(a_ref, w_ref, o_ref, *, act):
    y = _dot(a_ref[...], w_ref[...])
    if act == "gelu":
        y = _gelu_tanh(y)
    o_ref[...] = y.astype(o_ref.dtype)


def _mm(a, w, *, tm, tn, out_dtype, act=None):
    M, K = a.shape
    N = w.shape[1]
    tm, tn = min(tm, M), min(tn, N)
    return pl.pallas_call(
        functools.partial(_mm_kernel, act=act),
        grid=(N // tn, M // tm),
        in_specs=[pl.BlockSpec((tm, K), lambda n, m: (m, 0)),
                  pl.BlockSpec((K, tn), lambda n, m: (0, n))],
        out_specs=pl.BlockSpec((tm, tn), lambda n, m: (m, n)),
        out_shape=jax.ShapeDtypeStruct((M, N), out_dtype),
        compiler_params=_params(2),
        name="mm_" + (act or "plain"),
    )(a, w)


def _mm_res_ln_kernel(a_ref, w_ref, r_ref, g_ref, b_ref, of_ref, ob_ref):
    y = ALPHA * r_ref[...] + _dot(a_ref[...], w_ref[...])
    y = _layer_norm(y, g_ref[...], b_ref[...])
    of_ref[...] = y
    ob_ref[...] = y.astype(BF16)


def _mm_res_ln(a, w, res, g, b, *, tm):
    M, K = a.shape
    D = w.shape[1]
    tm = min(tm, M)
    return pl.pallas_call(
        _mm_res_ln_kernel,
        grid=(M // tm,),
        in_specs=[pl.BlockSpec((tm, K), lambda m: (m, 0)),
                  _resident((K, D), lambda m: (0, 0)),
                  pl.BlockSpec((tm, D), lambda m: (m, 0)),
                  _resident((1, D), lambda m: (0, 0)),
                  _resident((1, D), lambda m: (0, 0))],
        out_specs=[pl.BlockSpec((tm, D), lambda m: (m, 0)),
                   pl.BlockSpec((tm, D), lambda m: (m, 0))],
        out_shape=[jax.ShapeDtypeStruct((M, D), F32), jax.ShapeDtypeStruct((M, D), BF16)],
        compiler_params=_params(1),
        name="mm_res_ln",
    )(a, w, res, g.reshape(1, D), b.reshape(1, D))


def _swiglu_up_kernel(x_ref, wg_ref, wu_ref, o_ref):
    x = x_ref[...]
    g = _dot(x, wg_ref[...])
    u = _dot(x, wu_ref[...])
    o_ref[...] = (g * _sigmoid(g) * u).astype(o_ref.dtype)


def _swiglu_up(x, w_gu, *, tm, tn):
    M, K = x.shape
    F = w_gu.shape[1] // 2
    tm, tn = min(tm, M), min(tn, F)
    nf = F // tn
    return pl.pallas_call(
        _swiglu_up_kernel,
        grid=(nf, M // tm),
        in_specs=[pl.BlockSpec((tm, K), lambda n, m: (m, 0)),
                  pl.BlockSpec((K, tn), lambda n, m: (0, n)),
                  pl.BlockSpec((K, tn), lambda n, m: (0, nf + n))],
        out_specs=pl.BlockSpec((tm, tn), lambda n, m: (m, n)),
        out_shape=jax.ShapeDtypeStruct((M, F), BF16),
        compiler_params=_params(2),
        name="swiglu_up",
    )(x, w_gu, w_gu)


def _ple_kernel(xb_ref, wg_ref, p_ref, wp_ref, xf_ref, of_ref, ob_ref):
    gate = _sigmoid(_dot(xb_ref[...], wg_ref[...]))
    y = xf_ref[...] + gate * _dot(p_ref[...], wp_ref[...])
    of_ref[...] = y
    ob_ref[...] = y.astype(BF16)


def _ple(xb, xf, w_gate, p, w_proj, *, tm, tn):
    M, D = xf.shape
    P = p.shape[1]
    tm, tn = min(tm, M), min(tn, D)
    return pl.pallas_call(
        _ple_kernel,
        grid=(D // tn, M // tm),
        in_specs=[pl.BlockSpec((tm, D), lambda n, m: (m, 0)),
                  pl.BlockSpec((D, tn), lambda n, m: (0, n)),
                  pl.BlockSpec((tm, P), lambda n, m: (m, 0)),
                  pl.BlockSpec((P, tn), lambda n, m: (0, n)),
                  pl.BlockSpec((tm, tn), lambda n, m: (m, n))],
        out_specs=[pl.BlockSpec((tm, tn), lambda n, m: (m, n)),
                   pl.BlockSpec((tm, tn), lambda n, m: (m, n))],
        out_shape=[jax.ShapeDtypeStruct((M, D), F32), jax.ShapeDtypeStruct((M, D), BF16)],
        compiler_params=_params(2),
        name="ple",
    )(xb, w_gate, p, w_proj, xf)


def _split3_bf16(x):
    hi = x.astype(BF16)
    r = x - hi.astype(F32)
    mid = r.astype(BF16)
    lo = (r - mid.astype(F32)).astype(BF16)
    return hi, mid, lo


def _fox_gate_kernel(x_ref, wft_ref, bf_ref, c_ref, carry_ref, *, ts):
    @pl.when(pl.program_id(1) == 0)
    def _():
        carry_ref[...] = jnp.zeros_like(carry_ref)

    f = _dot_nt(wft_ref[...], x_ref[...]) + bf_ref[...]
    logf = jnp.minimum(f, 0.0) - jnp.log1p(jnp.exp(-jnp.abs(f)))
    row = lax.broadcasted_iota(jnp.int32, (ts, ts), 0)
    col = lax.broadcasted_iota(jnp.int32, (ts, ts), 1)
    tri = jnp.where(row <= col, 1.0, 0.0).astype(BF16)
    hi, mid, lo = _split3_bf16(logf)
    c = _dot(hi, tri) + _dot(mid, tri) + _dot(lo, tri) + carry_ref[...]
    c_ref[0] = c
    carry_ref[...] = c[:, ts - 1:ts]


def _fox_gate(xb, wft, b_f, *, batch, ts):
    T, D = xb.shape
    H = wft.shape[0]
    S = T // batch
    ts = min(ts, S)
    ns = S // ts
    return pl.pallas_call(
        functools.partial(_fox_gate_kernel, ts=ts),
        grid=(batch, ns),
        in_specs=[pl.BlockSpec((ts, D), lambda b, i: (b * ns + i, 0)),
                  _resident((H, D), lambda b, i: (0, 0)),
                  _resident((H, 1), lambda b, i: (0, 0))],
        out_specs=pl.BlockSpec((1, H, ts), lambda b, i: (b, 0, i)),
        out_shape=jax.ShapeDtypeStruct((batch, H, S), F32),
        scratch_shapes=[pltpu.VMEM((H, 1), F32)],
        compiler_params=_params(2),
        name="fox_gate",
    )(xb, wft, b_f.reshape(H, 1).astype(F32))


def _fox_attn_kernel(q_ref, k_ref, v_ref, c_ref, o_ref, *, tq, scale):
    qi = pl.program_id(2)
    q = (q_ref[...].astype(F32) * scale).astype(BF16)

    def block(j, carry, masked):
        m, l, acc = carry
        start = pl.multiple_of(j * tq, tq)
        k = k_ref[pl.ds(start, tq), :]
        v = v_ref[pl.ds(start, tq), :]
        s = _dot_nt(q, k) - c_ref[0, :, pl.ds(start, tq)]
        if masked:
            row = lax.broadcasted_iota(jnp.int32, (tq, tq), 0)
            col = lax.broadcasted_iota(jnp.int32, (tq, tq), 1)
            s = jnp.where(row >= col, s, NEG)
        m_new = jnp.maximum(m, jnp.max(s, axis=-1, keepdims=True))
        p = jnp.exp(s - m_new)
        a = jnp.exp(m - m_new)
        l = a * l + jnp.sum(p, axis=-1, keepdims=True)
        acc = a * acc + _dot(p.astype(BF16), v)
        return m_new, l, acc

    init = (jnp.full((tq, 1), NEG, F32), jnp.zeros((tq, 1), F32), jnp.zeros((tq, HEAD_DIM), F32))
    carry = lax.fori_loop(0, qi, lambda j, c: block(j, c, False), init)
    _, l, acc = block(qi, carry, True)
    o_ref[...] = (acc / l).astype(o_ref.dtype)


def _fox_attn(qkv, c, *, batch, tq):
    T, D3 = qkv.shape
    D = D3 // 3
    H = D // HEAD_DIM
    S = T // batch
    tq = min(tq, S)
    nq = S // tq
    c3 = c.reshape(batch * H, 1, S)
    return pl.pallas_call(
        functools.partial(_fox_attn_kernel, tq=tq, scale=HEAD_DIM ** -0.5),
        grid=(batch, H, nq),
        in_specs=[pl.BlockSpec((tq, HEAD_DIM), lambda b, h, i: (b * nq + i, h)),
                  pl.BlockSpec((S, HEAD_DIM), lambda b, h, i: (b, H + h)),
                  pl.BlockSpec((S, HEAD_DIM), lambda b, h, i: (b, 2 * H + h)),
                  pl.BlockSpec((1, 1, S), lambda b, h, i: (b * H + h, 0, 0))],
        out_specs=pl.BlockSpec((tq, HEAD_DIM), lambda b, h, i: (b * nq + i, h)),
        out_shape=jax.ShapeDtypeStruct((T, D), BF16),
        compiler_params=_params(3),
        name="fox_attn",
    )(qkv, qkv, qkv, c3)


def _gm_spatial_kernel(u_ref, v_ref, g_ref, b_ref, ws_ref, bs_ref, o_ref, *, n_chunks):
    C = GM_CHUNK
    vn = _layer_norm(v_ref[...].astype(F32), g_ref[...], b_ref[...]).astype(BF16)
    row = lax.broadcasted_iota(jnp.int32, (C, C), 0)
    col = lax.broadcasted_iota(jnp.int32, (C, C), 1)
    for grp in range(ws_ref.shape[0]):
        w = jnp.where(row >= col, ws_ref[grp], 0.0).astype(BF16)
        cols = slice(grp * C, (grp + 1) * C)
        for ch in range(n_chunks):
            rows = slice(ch * C, (ch + 1) * C)
            mixed = _dot(w, vn[rows, cols]) + bs_ref[:, cols]
            o_ref[rows, cols] = (u_ref[rows, cols].astype(F32) * mixed).astype(o_ref.dtype)


def _gm_spatial(z, ln_g, ln_b, w_s, bias_full, *, tm):
    T, W2 = z.shape
    W = W2 // 2
    G, C, _ = w_s.shape
    tm = min(tm, T)
    return pl.pallas_call(
        functools.partial(_gm_spatial_kernel, n_chunks=tm // C),
        grid=(T // tm,),
        in_specs=[pl.BlockSpec((tm, W), lambda i: (i, 0)),
                  pl.BlockSpec((tm, W), lambda i: (i, 1)),
                  _resident((1, W), lambda i: (0, 0)),
                  _resident((1, W), lambda i: (0, 0)),
                  _resident((G, C, C), lambda i: (0, 0, 0)),
                  _resident((C, W), lambda i: (0, 0))],
        out_specs=pl.BlockSpec((tm, W), lambda i: (i, 0)),
        out_shape=jax.ShapeDtypeStruct((T, W), BF16),
        compiler_params=_params(1),
        name="gm_spatial",
    )(z, z, ln_g.reshape(1, W), ln_b.reshape(1, W), w_s, bias_full)


def _router_kernel(x_ref, wrt_ref, o_ref, cnt_ref, carry_ref, *, tm, n_exp):
    @pl.when(pl.program_id(0) == 0)
    def _():
        carry_ref[...] = jnp.zeros_like(carry_ref)

    x = x_ref[...]
    xh = x.astype(BF16)
    xl = (x - xh.astype(F32)).astype(BF16)
    wh, wl = wrt_ref[0], wrt_ref[1]
    logits = _dot_nt(wh, xh) + _dot_nt(wh, xl) + _dot_nt(wl, xh)
    row = lax.broadcasted_iota(jnp.int32, logits.shape, 0)
    logits = jnp.where(row < n_exp, logits, NEG)
    top1 = jnp.max(logits, axis=0, keepdims=True)
    idx1 = jnp.min(jnp.where(logits == top1, row, EXPERT_ROWS), axis=0, keepdims=True)
    rest = jnp.where(row == idx1, NEG, logits)
    top2 = jnp.max(rest, axis=0, keepdims=True)
    idx2 = jnp.min(jnp.where(rest == top2, row, EXPERT_ROWS), axis=0, keepdims=True)
    e = jnp.exp(top2 - top1)
    gate1 = 1.0 / (1.0 + e)
    gate2 = e / (1.0 + e)

    sel = jnp.where((row == idx1) | (row == idx2), 1.0, 0.0)
    r = lax.broadcasted_iota(jnp.int32, (tm, tm), 0)
    c = lax.broadcasted_iota(jnp.int32, (tm, tm), 1)
    tri = jnp.where(r <= c, 1.0, 0.0).astype(BF16)
    incl = _dot(sel.astype(BF16), tri)
    excl = incl - sel + carry_ref[...]
    rank1 = jnp.sum(jnp.where(row == idx1, excl, 0.0), axis=0, keepdims=True)
    rank2 = jnp.sum(jnp.where(row == idx2, excl, 0.0), axis=0, keepdims=True)
    total = carry_ref[...] + incl[:, tm - 1:tm]
    carry_ref[...] = total
    cnt_ref[...] = jnp.broadcast_to(total, cnt_ref.shape)
    o_ref[...] = jnp.concatenate(
        [idx1.astype(F32), idx2.astype(F32), gate1, gate2, rank1, rank2,
         jnp.zeros((2, tm), F32)], axis=0)


def _router(xf, w_router, *, tm):
    T, D = xf.shape
    n_exp = w_router.shape[1]
    tm = min(tm, T)
    wt = jnp.zeros((EXPERT_ROWS, D), F32).at[:n_exp].set(w_router.T.astype(F32))
    wh = wt.astype(BF16)
    wl = (wt - wh.astype(F32)).astype(BF16)
    return pl.pallas_call(
        functools.partial(_router_kernel, tm=tm, n_exp=n_exp),
        grid=(T // tm,),
        in_specs=[pl.BlockSpec((tm, D), lambda i: (i, 0)),
                  _resident((2, EXPERT_ROWS, D), lambda i: (0, 0, 0))],
        out_specs=[pl.BlockSpec((8, tm), lambda i: (0, i)),
                   pl.BlockSpec((EXPERT_ROWS, 128), lambda i: (0, 0))],
        out_shape=[jax.ShapeDtypeStruct((8, T), F32),
                   jax.ShapeDtypeStruct((EXPERT_ROWS, 128), F32)],
        scratch_shapes=[pltpu.VMEM((EXPERT_ROWS, 1), F32)],
        compiler_params=_params(1),
        name="router",
    )(xf, jnp.stack([wh, wl]))


def _moe_up_kernel(te_ref, nu_ref, x_ref, wg_ref, wu_ref, o_ref):
    @pl.when(pl.program_id(1) < nu_ref[0])
    def _():
        x = x_ref[...].astype(BF16)
        g = _dot(x, wg_ref[...])
        u = _dot(x, wu_ref[...])
        o_ref[...] = (g * _sigmoid(g) * u).astype(o_ref.dtype)

    @pl.when(pl.program_id(1) >= nu_ref[0])
    def _():
        o_ref[...] = jnp.zeros_like(o_ref)


def _moe_up(xs, w_gu, tile_expert, n_used, *, tm, tn):
    P, K = xs.shape
    F = w_gu.shape[2] // 2
    tn = min(tn, F)
    nf = F // tn

    def row(n, m, te, nu):
        return jnp.minimum(m, nu[0] - 1)

    return pl.pallas_call(
        _moe_up_kernel,
        grid_spec=pltpu.PrefetchScalarGridSpec(
            num_scalar_prefetch=2,
            grid=(nf, P // tm),
            in_specs=[pl.BlockSpec((tm, K), lambda n, m, te, nu: (row(n, m, te, nu), 0)),
                      pl.BlockSpec((None, K, tn), lambda n, m, te, nu: (te[m], 0, n)),
                      pl.BlockSpec((None, K, tn), lambda n, m, te, nu: (te[m], 0, nf + n))],
            out_specs=pl.BlockSpec((tm, tn), lambda n, m, te, nu: (m, n))),
        out_shape=jax.ShapeDtypeStruct((P, F), BF16),
        compiler_params=_params(2),
        name="moe_up",
    )(tile_expert, n_used, xs, w_gu, w_gu)


def _moe_down_kernel(te_ref, nu_ref, h_ref, w_ref, o_ref):
    @pl.when(pl.program_id(1) < nu_ref[0])
    def _():
        o_ref[...] = _dot(h_ref[...], w_ref[...]).astype(o_ref.dtype)

    @pl.when(pl.program_id(1) >= nu_ref[0])
    def _():
        o_ref[...] = jnp.zeros_like(o_ref)


def _moe_down(hs, w_down, tile_expert, n_used, *, tm, tn):
    P, F = hs.shape
    D = w_down.shape[2]
    tn = min(tn, D)

    def row(n, m, te, nu):
        return jnp.minimum(m, nu[0] - 1)

    return pl.pallas_call(
        _moe_down_kernel,
        grid_spec=pltpu.PrefetchScalarGridSpec(
            num_scalar_prefetch=2,
            grid=(D // tn, P // tm),
            in_specs=[pl.BlockSpec((tm, F), lambda n, m, te, nu: (row(n, m, te, nu), 0)),
                      pl.BlockSpec((None, F, tn), lambda n, m, te, nu: (te[m], 0, n))],
            out_specs=pl.BlockSpec((tm, tn), lambda n, m, te, nu: (m, n))),
        out_shape=jax.ShapeDtypeStruct((P, D), F32),
        compiler_params=_params(2),
        name="moe_down",
    )(tile_expert, n_used, hs, w_down)


def _dispatch_kernel(pos_ref, pad_ref, x_hbm, xs_hbm, zero_ref, sem, *, tokens_per_step, n_tokens, n_groups):
    base = pl.program_id(0) * tokens_per_step

    def token_copy(t, k):
        dst = pos_ref[k * n_tokens + t]
        return pltpu.make_async_copy(x_hbm.at[pl.ds(t, 1)], xs_hbm.at[pl.ds(dst, 1)], sem)

    def pad_copy(dst):
        return pltpu.make_async_copy(zero_ref, xs_hbm.at[pl.ds(dst, 1)], sem)

    @pl.when(pl.program_id(0) == 0)
    def _():
        zero_ref[...] = jnp.zeros_like(zero_ref)
        for e in range(n_groups):
            first, count = pad_ref[e], pad_ref[n_groups + e]

            def start_pad(j, c):
                pad_copy(first + j).start()
                return c

            def wait_pad(j, c):
                pad_copy(first + j).wait()
                return c

            lax.fori_loop(0, count, start_pad, 0)
            lax.fori_loop(0, count, wait_pad, 0)

    def start_rows(j, c):
        for k in range(TOP_K):
            token_copy(base + j, k).start()
        return c

    def wait_rows(j, c):
        for k in range(TOP_K):
            token_copy(base + j, k).wait()
        return c

    lax.fori_loop(0, tokens_per_step, start_rows, 0)
    lax.fori_loop(0, tokens_per_step, wait_rows, 0)


def _dispatch(xf, pos_flat, pad_info, *, n_rows, tokens_per_step):
    T, D = xf.shape
    tokens_per_step = min(tokens_per_step, T)
    return pl.pallas_call(
        functools.partial(_dispatch_kernel, tokens_per_step=tokens_per_step, n_tokens=T,
                          n_groups=pad_info.shape[0] // 2),
        grid_spec=pltpu.PrefetchScalarGridSpec(
            num_scalar_prefetch=2,
            grid=(T // tokens_per_step,),
            in_specs=[pl.BlockSpec(memory_space=pl.ANY)],
            out_specs=pl.BlockSpec(memory_space=pl.ANY),
            scratch_shapes=[pltpu.VMEM((1, D), F32), pltpu.SemaphoreType.DMA(())]),
        out_shape=jax.ShapeDtypeStruct((n_rows, D), F32),
        compiler_params=_params(1),
        name="moe_dispatch",
    )(pos_flat, pad_info, xf)


def _combine_ln_kernel(pos_ref, r_ref, g1_ref, g2_ref, g_ref, b_ref, ys_hbm, of_ref, ob_ref, buf, sem,
                       *, tm, n_tokens):
    i = pl.program_id(0)

    def row_copy(tile, slot, j, k):
        src = pos_ref[k * n_tokens + tile * tm + j]
        return pltpu.make_async_copy(ys_hbm.at[pl.ds(src, 1)], buf.at[slot, k, pl.ds(j, 1)], sem.at[slot])

    def fetch(tile, slot):
        def body(j, c):
            for k in range(TOP_K):
                row_copy(tile, slot, j, k).start()
            return c
        lax.fori_loop(0, tm, body, 0)

    @pl.when(i == 0)
    def _():
        fetch(0, 0)

    @pl.when(i + 1 < pl.num_programs(0))
    def _():
        fetch(i + 1, (i + 1) % 2)

    slot = i % 2

    def wait_body(j, c):
        for k in range(TOP_K):
            row_copy(i, slot, j, k).wait()
        return c

    lax.fori_loop(0, tm, wait_body, 0)
    ch = g1_ref[...] * buf[slot, 0] + g2_ref[...] * buf[slot, 1]
    y = _layer_norm(ALPHA * r_ref[...] + ch, g_ref[...], b_ref[...])
    of_ref[...] = y
    ob_ref[...] = y.astype(BF16)


def _combine_ln(res, ys, pos_flat, g1, g2, g, b, *, tm):
    M, D = res.shape
    tm = min(tm, M)
    tile = pl.BlockSpec((tm, D), lambda m, pos: (m, 0))
    col = pl.BlockSpec((tm, 1), lambda m, pos: (m, 0))
    vec = _resident((1, D), lambda m, pos: (0, 0))
    return pl.pallas_call(
        functools.partial(_combine_ln_kernel, tm=tm, n_tokens=M),
        grid_spec=pltpu.PrefetchScalarGridSpec(
            num_scalar_prefetch=1,
            grid=(M // tm,),
            in_specs=[tile, col, col, vec, vec, pl.BlockSpec(memory_space=pl.ANY)],
            out_specs=[tile, tile],
            scratch_shapes=[pltpu.VMEM((2, TOP_K, tm, D), F32), pltpu.SemaphoreType.DMA((2,))]),
        out_shape=[jax.ShapeDtypeStruct((M, D), F32), jax.ShapeDtypeStruct((M, D), BF16)],
        compiler_params=_params(1),
        name="moe_combine_ln",
    )(pos_flat, res, g1, g2, g.reshape(1, D), b.reshape(1, D), ys)


def _moe_layer(xf, w_router, w_gu, w_down, ln_g, ln_b):
    T, D = xf.shape
    n_exp = w_router.shape[1]
    tm = min(MOE_TM, T)
    n_tiles = (T * TOP_K) // tm + n_exp

    route, counts = _router(xf, w_router, tm=512)
    idx = route[0:2].astype(jnp.int32)
    gates = route[2:4]
    rank = route[4:6].astype(jnp.int32)
    counts = counts[:n_exp, 0].astype(jnp.int32)
    tiles_per_expert = (counts + tm - 1) // tm
    tile_end = jnp.cumsum(tiles_per_expert)
    row_start = (tile_end - tiles_per_expert) * tm
    experts = jnp.arange(n_exp, dtype=jnp.int32)[:, None, None]
    pos = jnp.sum(jnp.where(idx[None] == experts, row_start[:, None, None], 0), axis=0) + rank
    pos_flat = pos.reshape(-1)
    pad_first = jnp.concatenate([row_start + counts, tile_end[-1:] * tm])
    pad_count = jnp.concatenate([tiles_per_expert * tm - counts, (n_tiles - tile_end[-1:]) * tm])
    pad_info = jnp.concatenate([pad_first, pad_count]).astype(jnp.int32)
    tile_ids = jnp.arange(n_tiles, dtype=jnp.int32)
    tile_expert = jnp.sum(tile_ids[:, None] >= tile_end[None, :], axis=1).astype(jnp.int32)
    last_expert = jnp.max(jnp.where(tiles_per_expert > 0, jnp.arange(n_exp, dtype=jnp.int32), 0))
    tile_expert = jnp.minimum(tile_expert, last_expert)
    n_used = tile_end[-1].reshape(1).astype(jnp.int32)

    xs = _dispatch(xf, pos_flat, pad_info, n_rows=n_tiles * tm, tokens_per_step=512)
    hs = _moe_up(xs, w_gu, tile_expert, n_used, tm=tm, tn=1792)
    ys = _moe_down(hs, w_down, tile_expert, n_used, tm=tm, tn=1024)
    return _combine_ln(xf, ys, pos_flat, gates[0].reshape(T, 1), gates[1].reshape(T, 1), ln_g, ln_b, tm=256)


def kernel(x, p, fox_w_in, fox_b_f, fox_w_o, gm_w_in, gm_ln_v_g, gm_ln_v_b, gm_w_s, gm_b_s, gm_w_o, ffn_w_gu, ffn_w_down, moe_w_router, moe_w_gu, moe_w_down, ln_mix_g, ln_mix_b, ln_ch_g, ln_ch_b, ple_w_proj, ple_w_gate):
    B, S, D = x.shape
    T = B * S
    H = D // HEAD_DIM
    xf = x.reshape(T, D)
    xb = xf.astype(BF16)
    pb = p.reshape(p.shape[0], T, p.shape[-1]).astype(BF16)

    w_in = fox_w_in[0]
    qkv = _mm(xb, w_in[:, :3 * D].astype(BF16), tm=1024, tn=1024, out_dtype=BF16)
    c = _fox_gate(xb, w_in[:, 3 * D:].T.astype(BF16), fox_b_f[0], batch=B, ts=512)
    attn = _fox_attn(qkv, c, batch=B, tq=512)
    xf, xb = _mm_res_ln(attn, fox_w_o[0].astype(BF16), xf, ln_mix_g[0], ln_mix_b[0], tm=256)
    hid = _swiglu_up(xb, ffn_w_gu[0].astype(BF16), tm=1024, tn=512)
    xf, xb = _mm_res_ln(hid, ffn_w_down[0].astype(BF16), xf, ln_ch_g[0], ln_ch_b[0], tm=256)
    xf, xb = _ple(xb, xf, ple_w_gate[0].astype(BF16), pb[0], ple_w_proj[0].astype(BF16), tm=512, tn=1024)

    z = _mm(xb, gm_w_in[0].astype(BF16), tm=1024, tn=1024, out_dtype=BF16, act="gelu")
    bias_full = jnp.repeat(gm_b_s[0].T.astype(F32), D // gm_b_s.shape[1], axis=1)
    y = _gm_spatial(z, gm_ln_v_g[0], gm_ln_v_b[0], gm_w_s[0], bias_full, tm=512)
    xf, xb = _mm_res_ln(y, gm_w_o[0].astype(BF16), xf, ln_mix_g[1], ln_mix_b[1], tm=256)
    xf, xb = _moe_layer(xf, moe_w_router[0], moe_w_gu[0].astype(BF16), moe_w_down[0].astype(BF16),
                        ln_ch_g[1], ln_ch_b[1])
    xf, _ = _ple(xb, xf, ple_w_gate[1].astype(BF16), pb[1], ple_w_proj[1].astype(BF16), tm=512, tn=1024)
    return xf.reshape(B, S, D)
```

```python
import functools

import jax
import jax.numpy as jnp
from jax import lax
from jax.experimental import pallas as pl
from jax.experimental.pallas import tpu as pltpu

F32 = jnp.float32
BF16 = jnp.bfloat16

LN_EPS = 1e-5
DEPTH = 2
ALPHA = (2.0 * DEPTH) ** 0.25
HEAD_DIM = 128
GM_CHUNK = 128
GM_GROUPS = 16
TOP_K = 2
NEG = -1e30
V7X_VMEM_LIMIT_BYTES = 56 * 2**20
EXPERT_ROWS = 16
MOE_TM = 256


def _params(n_axes):
    return pltpu.CompilerParams(dimension_semantics=("arbitrary",) * n_axes,
                                vmem_limit_bytes=V7X_VMEM_LIMIT_BYTES)


def _dot(a, b):
    return jnp.dot(a, b, preferred_element_type=F32)


def _dot_nt(a, b):
    return lax.dot_general(a, b, (((1,), (1,)), ((), ())), preferred_element_type=F32)


def _sigmoid(x):
    return 1.0 / (1.0 + jnp.exp(-x))


def _gelu_tanh(x):
    return 0.5 * x * (1.0 + jnp.tanh(0.7978845608028654 * (x + 0.044715 * (x * x * x))))


def _layer_norm(y, g, b):
    mu = jnp.mean(y, axis=-1, keepdims=True)
    d = y - mu
    var = jnp.mean(d * d, axis=-1, keepdims=True)
    return d * lax.rsqrt(var + LN_EPS) * g + b


def _resident(block_shape, index_map):
    return pl.BlockSpec(block_shape, index_map, pipeline_mode=pl.Buffered(1))


def _mm_kernel(a_ref, w_ref, o_ref, *, act):
    y = _dot(a_ref[...], w_ref[...])
    if act == "gelu":
        y = _gelu_tanh(y)
    o_ref[...] = y.astype(o_ref.dtype)


def _mm(a, w, *, tm, tn, out_dtype, act=None):
    M, K = a.shape
    N = w.shape[1]
    tm, tn = min(tm, M), min(tn, N)
    return pl.pallas_call(
        functools.partial(_mm_kernel, act=act),
        grid=(N // tn, M // tm),
        in_specs=[pl.BlockSpec((tm, K), lambda n, m: (m, 0)),
                  pl.BlockSpec((K, tn), lambda n, m: (0, n))],
        out_specs=pl.BlockSpec((tm, tn), lambda n, m: (m, n)),
        out_shape=jax.ShapeDtypeStruct((M, N), out_dtype),
        compiler_params=_params(2),
        name="mm_" + (act or "plain"),
    )(a, w)


def _mm_res_ln_kernel(a_ref, w_ref, r_ref, g_ref, b_ref, of_ref, ob_ref):
    y = ALPHA * r_ref[...] + _dot(a_ref[...], w_ref[...])
    y = _layer_norm(y, g_ref[...], b_ref[...])
    of_ref[...] = y
    ob_ref[...] = y.astype(BF16)


def _mm_res_ln(a, w, res, g, b, *, tm):
    M, K = a.shape
    D = w.shape[1]
    tm = min(tm, M)
    return pl.pallas_call(
        _mm_res_ln_kernel,
        grid=(M // tm,),
        in_specs=[pl.BlockSpec((tm, K), lambda m: (m, 0)),
                  _resident((K, D), lambda m: (0, 0)),
                  pl.BlockSpec((tm, D), lambda m: (m, 0)),
                  _resident((1, D), lambda m: (0, 0)),
                  _resident((1, D), lambda m: (0, 0))],
        out_specs=[pl.BlockSpec((tm, D), lambda m: (m, 0)),
                   pl.BlockSpec((tm, D), lambda m: (m, 0))],
        out_shape=[jax.ShapeDtypeStruct((M, D), F32), jax.ShapeDtypeStruct((M, D), BF16)],
        compiler_params=_params(1),
        name="mm_res_ln",
    )(a, w, res, g.reshape(1, D), b.reshape(1, D))


def _swiglu_up_kernel(x_ref, wg_ref, wu_ref, o_ref):
    x = x_ref[...]
    g = _dot(x, wg_ref[...])
    u = _dot(x, wu_ref[...])
    o_ref[...] = (g * _sigmoid(g) * u).astype(o_ref.dtype)


def _swiglu_up(x, w_gu, *, tm, tn):
    M, K = x.shape
    F = w_gu.shape[1] // 2
    tm, tn = min(tm, M), min(tn, F)
    nf = F // tn
    return pl.pallas_call(
        _swiglu_up_kernel,
        grid=(nf, M // tm),
        in_specs=[pl.BlockSpec((tm, K), lambda n, m: (m, 0)),
                  pl.BlockSpec((K, tn), lambda n, m: (0, n)),
                  pl.BlockSpec((K, tn), lambda n, m: (0, nf + n))],
        out_specs=pl.BlockSpec((tm, tn), lambda n, m: (m, n)),
        out_shape=jax.ShapeDtypeStruct((M, F), BF16),
        compiler_params=_params(2),
        name="swiglu_up",
    )(x, w_gu, w_gu)


def _ple_kernel(xb_ref, wg_ref, p_ref, wp_ref, xf_ref, of_ref, ob_ref):
    gate = _sigmoid(_dot(xb_ref[...], wg_ref[...]))
    y = xf_ref[...] + gate * _dot(p_ref[...], wp_ref[...])
    of_ref[...] = y
    ob_ref[...] = y.astype(BF16)


def _ple(xb, xf, w_gate, p, w_proj, *, tm, tn):
    M, D = xf.shape
    P = p.shape[1]
    tm, tn = min(tm, M), min(tn, D)
    return pl.pallas_call(
        _ple_kernel,
        grid=(D // tn, M // tm),
        in_specs=[pl.BlockSpec((tm, D), lambda n, m: (m, 0)),
                  pl.BlockSpec((D, tn), lambda n, m: (0, n)),
                  pl.BlockSpec((tm, P), lambda n, m: (m, 0)),
                  pl.BlockSpec((P, tn), lambda n, m: (0, n)),
                  pl.BlockSpec((tm, tn), lambda n, m: (m, n))],
        out_specs=[pl.BlockSpec((tm, tn), lambda n, m: (m, n)),
                   pl.BlockSpec((tm, tn), lambda n, m: (m, n))],
        out_shape=[jax.ShapeDtypeStruct((M, D), F32), jax.ShapeDtypeStruct((M, D), BF16)],
        compiler_params=_params(2),
        name="ple",
    )(xb, w_gate, p, w_proj, xf)


def _split3_bf16(x):
    hi = x.astype(BF16)
    r = x - hi.astype(F32)
    mid = r.astype(BF16)
    lo = (r - mid.astype(F32)).astype(BF16)
    return hi, mid, lo


def _fox_gate_kernel(x_ref, wft_ref, bf_ref, c_ref, carry_ref, *, ts):
    @pl.when(pl.program_id(1) == 0)
    def _():
        carry_ref[...] = jnp.zeros_like(carry_ref)

    f = _dot_nt(wft_ref[...], x_ref[...]) + bf_ref[...]
    logf = jnp.minimum(f, 0.0) - jnp.log1p(jnp.exp(-jnp.abs(f)))
    row = lax.broadcasted_iota(jnp.int32, (ts, ts), 0)
    col = lax.broadcasted_iota(jnp.int32, (ts, ts), 1)
    tri = jnp.where(row <= col, 1.0, 0.0).astype(BF16)
    hi, mid, lo = _split3_bf16(logf)
    c = _dot(hi, tri) + _dot(mid, tri) + _dot(lo, tri) + carry_ref[...]
    c_ref[0] = c
    carry_ref[...] = c[:, ts - 1:ts]


def _fox_gate(xb, wft, b_f, *, batch, ts):
    T, D = xb.shape
    H = wft.shape[0]
    S = T // batch
    ts = min(ts, S)
    ns = S // ts
    return pl.pallas_call(
        functools.partial(_fox_gate_kernel, ts=ts),
        grid=(batch, ns),
        in_specs=[pl.BlockSpec((ts, D), lambda b, i: (b * ns + i, 0)),
                  _resident((H, D), lambda b, i: (0, 0)),
                  _resident((H, 1), lambda b, i: (0, 0))],
        out_specs=pl.BlockSpec((1, H, ts), lambda b, i: (b, 0, i)),
        out_shape=jax.ShapeDtypeStruct((batch, H, S), F32),
        scratch_shapes=[pltpu.VMEM((H, 1), F32)],
        compiler_params=_params(2),
        name="fox_gate",
    )(xb, wft, b_f.reshape(H, 1).astype(F32))


def _fox_attn_kernel(q_ref, k_ref, v_ref, c_ref, o_ref, *, tq, n_sub, scale):
    qi = pl.program_id(2)
    sub = tq // n_sub
    log2e = 1.4426950408889634
    qs = [(q_ref[r * sub:(r + 1) * sub, :].astype(F32) * (scale * log2e)).astype(BF16) for r in range(n_sub)]

    def update(carry, q, k, v, cj, row_offset):
        m, l, acc = carry
        s = _dot_nt(q, k) - cj
        if row_offset is not None:
            row = lax.broadcasted_iota(jnp.int32, s.shape, 0) + row_offset
            col = lax.broadcasted_iota(jnp.int32, s.shape, 1)
            s = jnp.where(row >= col, s, NEG)
        m_new = jnp.maximum(m, jnp.max(s, axis=-1, keepdims=True))
        p = jnp.exp2(s - m_new)
        a = jnp.exp2(m - m_new)
        l = a * l + jnp.sum(p, axis=-1, keepdims=True)
        acc = a * acc + _dot(p.astype(BF16), v)
        return m_new, l, acc

    def full_block(j, carries):
        start = pl.multiple_of(j * tq, tq)
        k = k_ref[pl.ds(start, tq), :]
        v = v_ref[pl.ds(start, tq), :]
        cj = c_ref[0, :, pl.ds(start, tq)] * log2e
        return tuple(update(carries[r], qs[r], k, v, cj, None) for r in range(n_sub))

    init = tuple((jnp.full((sub, 1), NEG, F32), jnp.zeros((sub, 1), F32), jnp.zeros((sub, HEAD_DIM), F32))
                 for _ in range(n_sub))
    carries = lax.fori_loop(0, qi, full_block, init)

    start = pl.multiple_of(qi * tq, tq)
    for r in range(n_sub):
        width = (r + 1) * sub
        k = k_ref[pl.ds(start, width), :]
        v = v_ref[pl.ds(start, width), :]
        cj = c_ref[0, :, pl.ds(start, width)] * log2e
        _, l, acc = update(carries[r], qs[r], k, v, cj, r * sub)
        o_ref[r * sub:(r + 1) * sub, :] = (acc / l).astype(o_ref.dtype)


def _fox_attn(qkv, c, *, batch, tq, n_sub):
    T, D3 = qkv.shape
    D = D3 // 3
    H = D // HEAD_DIM
    S = T // batch
    tq = min(tq, S)
    nq = S // tq
    c3 = c.reshape(batch * H, 1, S)
    return pl.pallas_call(
        functools.partial(_fox_attn_kernel, tq=tq, n_sub=n_sub, scale=HEAD_DIM ** -0.5),
        grid=(batch, H, nq),
        in_specs=[pl.BlockSpec((tq, HEAD_DIM), lambda b, h, i: (b * nq + i, h)),
                  pl.BlockSpec((S, HEAD_DIM), lambda b, h, i: (b, H + h)),
                  pl.BlockSpec((S, HEAD_DIM), lambda b, h, i: (b, 2 * H + h)),
                  pl.BlockSpec((1, 1, S), lambda b, h, i: (b * H + h, 0, 0))],
        out_specs=pl.BlockSpec((tq, HEAD_DIM), lambda b, h, i: (b * nq + i, h)),
        out_shape=jax.ShapeDtypeStruct((T, D), BF16),
        compiler_params=_params(3),
        name="fox_attn",
    )(qkv, qkv, qkv, c3)


def _gm_spatial_kernel(u_ref, v_ref, g_ref, b_ref, ws_ref, bs_ref, o_ref, *, n_chunks):
    C = GM_CHUNK
    vn = _layer_norm(v_ref[...].astype(F32), g_ref[...], b_ref[...]).astype(BF16)
    row = lax.broadcasted_iota(jnp.int32, (C, C), 0)
    col = lax.broadcasted_iota(jnp.int32, (C, C), 1)
    for grp in range(ws_ref.shape[0]):
        w = jnp.where(row >= col, ws_ref[grp], 0.0).astype(BF16)
        cols = slice(grp * C, (grp + 1) * C)
        for ch in range(n_chunks):
            rows = slice(ch * C, (ch + 1) * C)
            mixed = _dot(w, vn[rows, cols]) + bs_ref[:, cols]
            o_ref[rows, cols] = (u_ref[rows, cols].astype(F32) * mixed).astype(o_ref.dtype)


def _gm_spatial(z, ln_g, ln_b, w_s, bias_full, *, tm):
    T, W2 = z.shape
    W = W2 // 2
    G, C, _ = w_s.shape
    tm = min(tm, T)
    return pl.pallas_call(
        functools.partial(_gm_spatial_kernel, n_chunks=tm // C),
        grid=(T // tm,),
        in_specs=[pl.BlockSpec((tm, W), lambda i: (i, 0)),
                  pl.BlockSpec((tm, W), lambda i: (i, 1)),
                  _resident((1, W), lambda i: (0, 0)),
                  _resident((1, W), lambda i: (0, 0)),
                  _resident((G, C, C), lambda i: (0, 0, 0)),
                  _resident((C, W), lambda i: (0, 0))],
        out_specs=pl.BlockSpec((tm, W), lambda i: (i, 0)),
        out_shape=jax.ShapeDtypeStruct((T, W), BF16),
        compiler_params=_params(1),
        name="gm_spatial",
    )(z, z, ln_g.reshape(1, W), ln_b.reshape(1, W), w_s, bias_full)


def _router_kernel(x_ref, wrt_ref, o_ref, cnt_ref, carry_ref, *, tm, n_exp):
    @pl.when(pl.program_id(0) == 0)
    def _():
        carry_ref[...] = jnp.zeros_like(carry_ref)

    x = x_ref[...]
    xh = x.astype(BF16)
    xl = (x - xh.astype(F32)).astype(BF16)
    wh, wl = wrt_ref[0], wrt_ref[1]
    logits = _dot_nt(wh, xh) + _dot_nt(wh, xl) + _dot_nt(wl, xh)
    row = lax.broadcasted_iota(jnp.int32, logits.shape, 0)
    logits = jnp.where(row < n_exp, logits, NEG)
    top1 = jnp.max(logits, axis=0, keepdims=True)
    idx1 = jnp.min(jnp.where(logits == top1, row, EXPERT_ROWS), axis=0, keepdims=True)
    rest = jnp.where(row == idx1, NEG, logits)
    top2 = jnp.max(rest, axis=0, keepdims=True)
    idx2 = jnp.min(jnp.where(rest == top2, row, EXPERT_ROWS), axis=0, keepdims=True)
    e = jnp.exp(top2 - top1)
    gate1 = 1.0 / (1.0 + e)
    gate2 = e / (1.0 + e)

    sel = jnp.where((row == idx1) | (row == idx2), 1.0, 0.0)
    r = lax.broadcasted_iota(jnp.int32, (tm, tm), 0)
    c = lax.broadcasted_iota(jnp.int32, (tm, tm), 1)
    tri = jnp.where(r <= c, 1.0, 0.0).astype(BF16)
    incl = _dot(sel.astype(BF16), tri)
    excl = incl - sel + carry_ref[...]
    rank1 = jnp.sum(jnp.where(row == idx1, excl, 0.0), axis=0, keepdims=True)
    rank2 = jnp.sum(jnp.where(row == idx2, excl, 0.0), axis=0, keepdims=True)
    total = carry_ref[...] + incl[:, tm - 1:tm]
    carry_ref[...] = total
    cnt_ref[...] = jnp.broadcast_to(total, cnt_ref.shape)
    o_ref[...] = jnp.concatenate(
        [idx1.astype(F32), idx2.astype(F32), gate1, gate2, rank1, rank2,
         jnp.zeros((2, tm), F32)], axis=0)


def _router(xf, w_router, *, tm):
    T, D = xf.shape
    n_exp = w_router.shape[1]
    tm = min(tm, T)
    wt = jnp.zeros((EXPERT_ROWS, D), F32).at[:n_exp].set(w_router.T.astype(F32))
    wh = wt.astype(BF16)
    wl = (wt - wh.astype(F32)).astype(BF16)
    return pl.pallas_call(
        functools.partial(_router_kernel, tm=tm, n_exp=n_exp),
        grid=(T // tm,),
        in_specs=[pl.BlockSpec((tm, D), lambda i: (i, 0)),
                  _resident((2, EXPERT_ROWS, D), lambda i: (0, 0, 0))],
        out_specs=[pl.BlockSpec((8, tm), lambda i: (0, i)),
                   pl.BlockSpec((EXPERT_ROWS, 128), lambda i: (0, 0))],
        out_shape=[jax.ShapeDtypeStruct((8, T), F32),
                   jax.ShapeDtypeStruct((EXPERT_ROWS, 128), F32)],
        scratch_shapes=[pltpu.VMEM((EXPERT_ROWS, 1), F32)],
        compiler_params=_params(1),
        name="router",
    )(xf, jnp.stack([wh, wl]))


def _grouped_kernel(first_ref, count_ref, *refs, tm, tn, n_tiles, gated):
    n_w = 2 if gated else 1
    w_refs, (x_hbm, o_hbm), w_bf = refs[:n_w], refs[n_w:n_w + 2], refs[n_w + 2:2 * n_w + 2]
    xbuf, obuf, in_sem, out_sem = refs[2 * n_w + 2:]
    n, e = pl.program_id(0), pl.program_id(1)
    first, count = first_ref[e], count_ref[e]
    col = pl.multiple_of(n * tn, 128)

    def in_copy(tile, slot):
        rows = pl.ds(pl.multiple_of(tile * tm, tm), tm)
        return pltpu.make_async_copy(x_hbm.at[rows], xbuf.at[slot], in_sem.at[slot])

    def out_copy(tile, slot):
        rows = pl.ds(pl.multiple_of(tile * tm, tm), tm)
        return pltpu.make_async_copy(obuf.at[slot], o_hbm.at[rows, pl.ds(col, tn)], out_sem.at[slot])

    @pl.when(count > 0)
    def _():
        for w_ref, w in zip(w_refs, w_bf):
            w[...] = w_ref[...].astype(BF16)
        in_copy(first, 0).start()

        def body(i, c):
            slot = i % 2
            in_copy(first + i, slot).wait()

            @pl.when(i + 1 < count)
            def _():
                in_copy(first + i + 1, 1 - slot).start()

            @pl.when(i >= 2)
            def _():
                out_copy(first + i - 2, slot).wait()

            x = xbuf[slot]
            if x.dtype != BF16:
                x = x.astype(BF16)
            y = _dot(x, w_bf[0][...])
            if gated:
                y = y * _sigmoid(y) * _dot(x, w_bf[1][...])
            obuf[slot] = y.astype(obuf.dtype)
            out_copy(first + i, slot).start()
            return c

        lax.fori_loop(0, count, body, 0)

        @pl.when(count >= 2)
        def _():
            out_copy(first + count - 2, count % 2).wait()

        out_copy(first + count - 1, (count - 1) % 2).wait()

    @pl.when(e == pl.num_programs(1) - 1)
    def _():
        obuf[0] = jnp.zeros(obuf.shape[1:], obuf.dtype)

        def start_zero(t, c):
            out_copy(t, 0).start()
            return c

        def wait_zero(t, c):
            out_copy(t, 0).wait()
            return c

        lax.fori_loop(first + count, n_tiles, start_zero, 0)
        lax.fori_loop(first + count, n_tiles, wait_zero, 0)


def _grouped_mm(xs, w, first_tile, tile_count, *, tm, tn, gated, out_dtype):
    P, K = xs.shape
    E = w.shape[0]
    N = w.shape[2] // 2 if gated else w.shape[2]
    tn = min(tn, N)
    nb = N // tn
    w_specs = [pl.BlockSpec((None, K, tn), lambda n, e, first, count: (e, 0, n))]
    if gated:
        w_specs.append(pl.BlockSpec((None, K, tn), lambda n, e, first, count: (e, 0, nb + n)))
    return pl.pallas_call(
        functools.partial(_grouped_kernel, tm=tm, tn=tn, n_tiles=P // tm, gated=gated),
        grid_spec=pltpu.PrefetchScalarGridSpec(
            num_scalar_prefetch=2,
            grid=(nb, E),
            in_specs=w_specs + [pl.BlockSpec(memory_space=pl.ANY)],
            out_specs=pl.BlockSpec(memory_space=pl.ANY),
            scratch_shapes=[pltpu.VMEM((K, tn), BF16)] * len(w_specs)
            + [pltpu.VMEM((2, tm, K), xs.dtype), pltpu.VMEM((2, tm, tn), out_dtype),
               pltpu.SemaphoreType.DMA((2,)), pltpu.SemaphoreType.DMA((2,))]),
        out_shape=jax.ShapeDtypeStruct((P, N), out_dtype),
        compiler_params=_params(2),
        name="moe_up" if gated else "moe_down",
    )(first_tile, tile_count, *([w] * len(w_specs)), xs)


def _dispatch_kernel(pos_ref, pad_ref, x_ref, xs_hbm, zero_ref, sem, *, tokens_per_step, n_tokens, n_groups):
    base = pl.program_id(0) * tokens_per_step

    def token_copy(j, k):
        dst = pos_ref[k * n_tokens + base + j]
        return pltpu.make_async_copy(x_ref.at[pl.ds(j, 1)], xs_hbm.at[pl.ds(dst, 1)], sem)

    def pad_copy(dst):
        return pltpu.make_async_copy(zero_ref, xs_hbm.at[pl.ds(dst, 1)], sem)

    @pl.when(pl.program_id(0) == 0)
    def _():
        zero_ref[...] = jnp.zeros_like(zero_ref)
        for e in range(n_groups):
            first, count = pad_ref[e], pad_ref[n_groups + e]

            def start_pad(j, c):
                pad_copy(first + j).start()
                return c

            def wait_pad(j, c):
                pad_copy(first + j).wait()
                return c

            lax.fori_loop(0, count, start_pad, 0)
            lax.fori_loop(0, count, wait_pad, 0)

    def start_rows(j, c):
        for k in range(TOP_K):
            token_copy(j, k).start()
        return c

    def wait_rows(j, c):
        for k in range(TOP_K):
            token_copy(j, k).wait()
        return c

    lax.fori_loop(0, tokens_per_step, start_rows, 0)
    lax.fori_loop(0, tokens_per_step, wait_rows, 0)


def _dispatch(xf, pos_flat, pad_info, *, n_rows, tokens_per_step):
    T, D = xf.shape
    tokens_per_step = min(tokens_per_step, T)
    return pl.pallas_call(
        functools.partial(_dispatch_kernel, tokens_per_step=tokens_per_step, n_tokens=T,
                          n_groups=pad_info.shape[0] // 2),
        grid_spec=pltpu.PrefetchScalarGridSpec(
            num_scalar_prefetch=2,
            grid=(T // tokens_per_step,),
            in_specs=[pl.BlockSpec((tokens_per_step, D), lambda i, pos, pad: (i, 0))],
            out_specs=pl.BlockSpec(memory_space=pl.ANY),
            scratch_shapes=[pltpu.VMEM((1, D), F32), pltpu.SemaphoreType.DMA(())]),
        out_shape=jax.ShapeDtypeStruct((n_rows, D), F32),
        compiler_params=_params(1),
        name="moe_dispatch",
    )(pos_flat, pad_info, xf)


def _combine_ln_kernel(pos_ref, r_ref, g1_ref, g2_ref, g_ref, b_ref, ys_hbm, of_ref, ob_ref, buf, sem,
                       *, tm, n_tokens):
    i = pl.program_id(0)

    def row_copy(tile, slot, j, k):
        src = pos_ref[k * n_tokens + tile * tm + j]
        return pltpu.make_async_copy(ys_hbm.at[pl.ds(src, 1)], buf.at[slot, k, pl.ds(j, 1)], sem.at[slot])

    def fetch(tile, slot):
        def body(j, c):
            for k in range(TOP_K):
                row_copy(tile, slot, j, k).start()
            return c
        lax.fori_loop(0, tm, body, 0)

    @pl.when(i == 0)
    def _():
        fetch(0, 0)

    @pl.when(i + 1 < pl.num_programs(0))
    def _():
        fetch(i + 1, (i + 1) % 2)

    slot = i % 2

    def wait_body(j, c):
        for k in range(TOP_K):
            row_copy(i, slot, j, k).wait()
        return c

    lax.fori_loop(0, tm, wait_body, 0)
    ch = g1_ref[...] * buf[slot, 0] + g2_ref[...] * buf[slot, 1]
    y = _layer_norm(ALPHA * r_ref[...] + ch, g_ref[...], b_ref[...])
    of_ref[...] = y
    ob_ref[...] = y.astype(BF16)


def _combine_ln(res, ys, pos_flat, g1, g2, g, b, *, tm):
    M, D = res.shape
    tm = min(tm, M)
    tile = pl.BlockSpec((tm, D), lambda m, pos: (m, 0))
    col = pl.BlockSpec((tm, 1), lambda m, pos: (m, 0))
    vec = _resident((1, D), lambda m, pos: (0, 0))
    return pl.pallas_call(
        functools.partial(_combine_ln_kernel, tm=tm, n_tokens=M),
        grid_spec=pltpu.PrefetchScalarGridSpec(
            num_scalar_prefetch=1,
            grid=(M // tm,),
            in_specs=[tile, col, col, vec, vec, pl.BlockSpec(memory_space=pl.ANY)],
            out_specs=[tile, tile],
            scratch_shapes=[pltpu.VMEM((2, TOP_K, tm, D), F32), pltpu.SemaphoreType.DMA((2,))]),
        out_shape=[jax.ShapeDtypeStruct((M, D), F32), jax.ShapeDtypeStruct((M, D), BF16)],
        compiler_params=_params(1),
        name="moe_combine_ln",
    )(pos_flat, res, g1, g2, g.reshape(1, D), b.reshape(1, D), ys)


def _moe_layer(xf, w_router, w_gu, w_down, ln_g, ln_b):
    T, D = xf.shape
    n_exp = w_router.shape[1]
    tm = min(MOE_TM, T)
    n_tiles = (T * TOP_K) // tm + n_exp

    route, counts = _router(xf, w_router, tm=512)
    idx = route[0:2].astype(jnp.int32)
    gates = route[2:4]
    rank = route[4:6].astype(jnp.int32)
    counts = counts[:n_exp, 0].astype(jnp.int32)
    tiles_per_expert = (counts + tm - 1) // tm
    tile_end = jnp.cumsum(tiles_per_expert)
    row_start = (tile_end - tiles_per_expert) * tm
    experts = jnp.arange(n_exp, dtype=jnp.int32)[:, None, None]
    pos = jnp.sum(jnp.where(idx[None] == experts, row_start[:, None, None], 0), axis=0) + rank
    pos_flat = pos.reshape(-1)
    pad_first = jnp.concatenate([row_start + counts, tile_end[-1:] * tm])
    pad_count = jnp.concatenate([tiles_per_expert * tm - counts, (n_tiles - tile_end[-1:]) * tm])
    pad_info = jnp.concatenate([pad_first, pad_count]).astype(jnp.int32)
    first_tile = (tile_end - tiles_per_expert).astype(jnp.int32)
    tile_count = tiles_per_expert.astype(jnp.int32)

    xs = _dispatch(xf, pos_flat, pad_info, n_rows=n_tiles * tm, tokens_per_step=512)
    hs = _grouped_mm(xs, w_gu, first_tile, tile_count, tm=tm, tn=896, gated=True, out_dtype=BF16)
    ys = _grouped_mm(hs, w_down, first_tile, tile_count, tm=tm, tn=512, gated=False, out_dtype=F32)
    return _combine_ln(xf, ys, pos_flat, gates[0].reshape(T, 1), gates[1].reshape(T, 1), ln_g, ln_b, tm=256)


def kernel(x, p, fox_w_in, fox_b_f, fox_w_o, gm_w_in, gm_ln_v_g, gm_ln_v_b, gm_w_s, gm_b_s, gm_w_o, ffn_w_gu, ffn_w_down, moe_w_router, moe_w_gu, moe_w_down, ln_mix_g, ln_mix_b, ln_ch_g, ln_ch_b, ple_w_proj, ple_w_gate):
    B, S, D = x.shape
    T = B * S
    H = D // HEAD_DIM
    xf = x.reshape(T, D)
    xb = xf.astype(BF16)
    pb = p.reshape(p.shape[0], T, p.shape[-1]).astype(BF16)

    w_in = fox_w_in[0]
    qkv = _mm(xb, w_in[:, :3 * D].astype(BF16), tm=1024, tn=1024, out_dtype=BF16)
    c = _fox_gate(xb, w_in[:, 3 * D:].T.astype(BF16), fox_b_f[0], batch=B, ts=512)
    attn = _fox_attn(qkv, c, batch=B, tq=1024, n_sub=4)
    xf, xb = _mm_res_ln(attn, fox_w_o[0].astype(BF16), xf, ln_mix_g[0], ln_mix_b[0], tm=256)
    hid = _swiglu_up(xb, ffn_w_gu[0].astype(BF16), tm=1024, tn=512)
    xf, xb = _mm_res_ln(hid, ffn_w_down[0].astype(BF16), xf, ln_ch_g[0], ln_ch_b[0], tm=256)
    xf, xb = _ple(xb, xf, ple_w_gate[0].astype(BF16), pb[0], ple_w_proj[0].astype(BF16), tm=512, tn=1024)

    z = _mm(xb, gm_w_in[0].astype(BF16), tm=1024, tn=1024, out_dtype=BF16, act="gelu")
    bias_full = jnp.repeat(gm_b_s[0].T.astype(F32), D // gm_b_s.shape[1], axis=1)
    y = _gm_spatial(z, gm_ln_v_g[0], gm_ln_v_b[0], gm_w_s[0], bias_full, tm=512)
    xf, xb = _mm_res_ln(y, gm_w_o[0].astype(BF16), xf, ln_mix_g[1], ln_mix_b[1], tm=256)
    xf, xb = _moe_layer(xf, moe_w_router[0], moe_w_gu[0], moe_w_down[0],
                        ln_ch_g[1], ln_ch_b[1])
    xf, _ = _ple(xb, xf, ple_w_gate[1].astype(BF16), pb[1], ple_w_proj[1].astype(BF16), tm=512, tn=1024)
    return xf.reshape(B, S, D)
```

```python
import functools

import jax
import jax.numpy as jnp
from jax import lax
from jax.experimental import pallas as pl
from jax.experimental.pallas import tpu as pltpu

F32 = jnp.float32
BF16 = jnp.bfloat16

LN_EPS = 1e-5
DEPTH = 2
ALPHA = (2.0 * DEPTH) ** 0.25
HEAD_DIM = 128
GM_CHUNK = 128
GM_GROUPS = 16
TOP_K = 2
NEG = -1e30
V7X_VMEM_LIMIT_BYTES = 56 * 2**20
EXPERT_ROWS = 16
MOE_TM = 256


def _params(n_axes):
    return pltpu.CompilerParams(dimension_semantics=("arbitrary",) * n_axes,
                                vmem_limit_bytes=V7X_VMEM_LIMIT_BYTES)


def _dot(a, b):
    return jnp.dot(a, b, preferred_element_type=F32)


def _dot_nt(a, b):
    return lax.dot_general(a, b, (((1,), (1,)), ((), ())), preferred_element_type=F32)


def _sigmoid(x):
    return 1.0 / (1.0 + jnp.exp(-x))


def _gelu_tanh(x):
    return 0.5 * x * (1.0 + jnp.tanh(0.7978845608028654 * (x + 0.044715 * (x * x * x))))


def _layer_norm(y, g, b):
    mu = jnp.mean(y, axis=-1, keepdims=True)
    d = y - mu
    var = jnp.mean(d * d, axis=-1, keepdims=True)
    return d * lax.rsqrt(var + LN_EPS) * g + b


def _resident(block_shape, index_map):
    return pl.BlockSpec(block_shape, index_map, pipeline_mode=pl.Buffered(1))


def _mm_kernel(a_ref, w_ref, o_ref, *, act):
    y = _dot(a_ref[...], w_ref[...])
    if act == "gelu":
        y = _gelu_tanh(y)
    o_ref[...] = y.astype(o_ref.dtype)


def _mm(a, w, *, tm, tn, out_dtype, act=None):
    M, K = a.shape
    N = w.shape[1]
    tm, tn = min(tm, M), min(tn, N)
    return pl.pallas_call(
        functools.partial(_mm_kernel, act=act),
        grid=(N // tn, M // tm),
        in_specs=[pl.BlockSpec((tm, K), lambda n, m: (m, 0)),
                  pl.BlockSpec((K, tn), lambda n, m: (0, n))],
        out_specs=pl.BlockSpec((tm, tn), lambda n, m: (m, n)),
        out_shape=jax.ShapeDtypeStruct((M, N), out_dtype),
        compiler_params=_params(2),
        name="mm_" + (act or "plain"),
    )(a, w)


def _mm_res_ln_kernel(a_ref, w_ref, r_ref, g_ref, b_ref, of_ref, ob_ref):
    y = ALPHA * r_ref[...] + _dot(a_ref[...], w_ref[...])
    y = _layer_norm(y, g_ref[...], b_ref[...])
    of_ref[...] = y
    ob_ref[...] = y.astype(BF16)


def _mm_res_ln(a, w, res, g, b, *, tm):
    M, K = a.shape
    D = w.shape[1]
    tm = min(tm, M)
    return pl.pallas_call(
        _mm_res_ln_kernel,
        grid=(M // tm,),
        in_specs=[pl.BlockSpec((tm, K), lambda m: (m, 0)),
                  _resident((K, D), lambda m: (0, 0)),
                  pl.BlockSpec((tm, D), lambda m: (m, 0)),
                  _resident((1, D), lambda m: (0, 0)),
                  _resident((1, D), lambda m: (0, 0))],
        out_specs=[pl.BlockSpec((tm, D), lambda m: (m, 0)),
                   pl.BlockSpec((tm, D), lambda m: (m, 0))],
        out_shape=[jax.ShapeDtypeStruct((M, D), F32), jax.ShapeDtypeStruct((M, D), BF16)],
        compiler_params=_params(1),
        name="mm_res_ln",
    )(a, w, res, g.reshape(1, D), b.reshape(1, D))


def _swiglu_up_kernel(x_ref, wg_ref, wu_ref, o_ref):
    x = x_ref[...]
    g = _dot(x, wg_ref[...])
    u = _dot(x, wu_ref[...])
    o_ref[...] = (g * _sigmoid(g) * u).astype(o_ref.dtype)


def _swiglu_up(x, w_gu, *, tm, tn):
    M, K = x.shape
    F = w_gu.shape[1] // 2
    tm, tn = min(tm, M), min(tn, F)
    nf = F // tn
    return pl.pallas_call(
        _swiglu_up_kernel,
        grid=(nf, M // tm),
        in_specs=[pl.BlockSpec((tm, K), lambda n, m: (m, 0)),
                  pl.BlockSpec((K, tn), lambda n, m: (0, n)),
                  pl.BlockSpec((K, tn), lambda n, m: (0, nf + n))],
        out_specs=pl.BlockSpec((tm, tn), lambda n, m: (m, n)),
        out_shape=jax.ShapeDtypeStruct((M, F), BF16),
        compiler_params=_params(2),
        name="swiglu_up",
    )(x, w_gu, w_gu)


def _ple_kernel(xb_ref, wg_ref, p_ref, wp_ref, xf_ref, of_ref, ob_ref):
    gate = _sigmoid(_dot(xb_ref[...], wg_ref[...]))
    y = xf_ref[...] + gate * _dot(p_ref[...], wp_ref[...])
    of_ref[...] = y
    ob_ref[...] = y.astype(BF16)


def _ple(xb, xf, w_gate, p, w_proj, *, tm, tn):
    M, D = xf.shape
    P = p.shape[1]
    tm, tn = min(tm, M), min(tn, D)
    return pl.pallas_call(
        _ple_kernel,
        grid=(D // tn, M // tm),
        in_specs=[pl.BlockSpec((tm, D), lambda n, m: (m, 0)),
                  pl.BlockSpec((D, tn), lambda n, m: (0, n)),
                  pl.BlockSpec((tm, P), lambda n, m: (m, 0)),
                  pl.BlockSpec((P, tn), lambda n, m: (0, n)),
                  pl.BlockSpec((tm, tn), lambda n, m: (m, n))],
        out_specs=[pl.BlockSpec((tm, tn), lambda n, m: (m, n)),
                   pl.BlockSpec((tm, tn), lambda n, m: (m, n))],
        out_shape=[jax.ShapeDtypeStruct((M, D), F32), jax.ShapeDtypeStruct((M, D), BF16)],
        compiler_params=_params(2),
        name="ple",
    )(xb, w_gate, p, w_proj, xf)


def _split3_bf16(x):
    hi = x.astype(BF16)
    r = x - hi.astype(F32)
    mid = r.astype(BF16)
    lo = (r - mid.astype(F32)).astype(BF16)
    return hi, mid, lo


def _fox_gate_kernel(x_ref, wft_ref, bf_ref, c_ref, carry_ref, *, ts):
    @pl.when(pl.program_id(1) == 0)
    def _():
        carry_ref[...] = jnp.zeros_like(carry_ref)

    f = _dot_nt(wft_ref[...], x_ref[...]) + bf_ref[...]
    logf = jnp.minimum(f, 0.0) - jnp.log1p(jnp.exp(-jnp.abs(f)))
    row = lax.broadcasted_iota(jnp.int32, (ts, ts), 0)
    col = lax.broadcasted_iota(jnp.int32, (ts, ts), 1)
    tri = jnp.where(row <= col, 1.0, 0.0).astype(BF16)
    hi, mid, lo = _split3_bf16(logf)
    c = _dot(hi, tri) + _dot(mid, tri) + _dot(lo, tri) + carry_ref[...]
    c_ref[0] = c
    carry_ref[...] = c[:, ts - 1:ts]


def _fox_gate(xb, wft, b_f, *, batch, ts):
    T, D = xb.shape
    H = wft.shape[0]
    S = T // batch
    ts = min(ts, S)
    ns = S // ts
    return pl.pallas_call(
        functools.partial(_fox_gate_kernel, ts=ts),
        grid=(batch, ns),
        in_specs=[pl.BlockSpec((ts, D), lambda b, i: (b * ns + i, 0)),
                  _resident((H, D), lambda b, i: (0, 0)),
                  _resident((H, 1), lambda b, i: (0, 0))],
        out_specs=pl.BlockSpec((1, H, ts), lambda b, i: (b, 0, i)),
        out_shape=jax.ShapeDtypeStruct((batch, H, S), F32),
        scratch_shapes=[pltpu.VMEM((H, 1), F32)],
        compiler_params=_params(2),
        name="fox_gate",
    )(xb, wft, b_f.reshape(H, 1).astype(F32))


def _fox_attn_kernel(q_ref, k_ref, v_ref, c_ref, o_ref, *, tq, n_sub, scale):
    qi = pl.program_id(2)
    sub = tq // n_sub
    log2e = 1.4426950408889634
    qs = [(q_ref[r * sub:(r + 1) * sub, :].astype(F32) * (scale * log2e)).astype(BF16) for r in range(n_sub)]

    def update(carry, q, k, v, cj, row_offset):
        m, l, acc = carry
        s = _dot_nt(q, k) - cj
        if row_offset is not None:
            row = lax.broadcasted_iota(jnp.int32, s.shape, 0) + row_offset
            col = lax.broadcasted_iota(jnp.int32, s.shape, 1)
            s = jnp.where(row >= col, s, NEG)
        m_new = jnp.maximum(m, jnp.max(s, axis=-1, keepdims=True))
        p = jnp.exp2(s - m_new)
        a = jnp.exp2(m - m_new)
        l = a * l + jnp.sum(p, axis=-1, keepdims=True)
        acc = a * acc + _dot(p.astype(BF16), v)
        return m_new, l, acc

    def full_block(j, carries):
        start = pl.multiple_of(j * tq, tq)
        k = k_ref[pl.ds(start, tq), :]
        v = v_ref[pl.ds(start, tq), :]
        cj = c_ref[0, :, pl.ds(start, tq)] * log2e
        return tuple(update(carries[r], qs[r], k, v, cj, None) for r in range(n_sub))

    init = tuple((jnp.full((sub, 1), NEG, F32), jnp.zeros((sub, 1), F32), jnp.zeros((sub, HEAD_DIM), F32))
                 for _ in range(n_sub))
    carries = lax.fori_loop(0, qi, full_block, init)

    start = pl.multiple_of(qi * tq, tq)
    for r in range(n_sub):
        width = (r + 1) * sub
        k = k_ref[pl.ds(start, width), :]
        v = v_ref[pl.ds(start, width), :]
        cj = c_ref[0, :, pl.ds(start, width)] * log2e
        _, l, acc = update(carries[r], qs[r], k, v, cj, r * sub)
        o_ref[r * sub:(r + 1) * sub, :] = (acc / l).astype(o_ref.dtype)


def _fox_attn(qkv, c, *, batch, tq, n_sub):
    T, D3 = qkv.shape
    D = D3 // 3
    H = D // HEAD_DIM
    S = T // batch
    tq = min(tq, S)
    nq = S // tq
    c3 = c.reshape(batch * H, 1, S)
    return pl.pallas_call(
        functools.partial(_fox_attn_kernel, tq=tq, n_sub=n_sub, scale=HEAD_DIM ** -0.5),
        grid=(batch, H, nq),
        in_specs=[pl.BlockSpec((tq, HEAD_DIM), lambda b, h, i: (b * nq + i, h)),
                  pl.BlockSpec((S, HEAD_DIM), lambda b, h, i: (b, H + h)),
                  pl.BlockSpec((S, HEAD_DIM), lambda b, h, i: (b, 2 * H + h)),
                  pl.BlockSpec((1, 1, S), lambda b, h, i: (b * H + h, 0, 0))],
        out_specs=pl.BlockSpec((tq, HEAD_DIM), lambda b, h, i: (b * nq + i, h)),
        out_shape=jax.ShapeDtypeStruct((T, D), BF16),
        compiler_params=_params(3),
        name="fox_attn",
    )(qkv, qkv, qkv, c3)


def _gm_spatial_kernel(u_ref, v_ref, g_ref, b_ref, ws_ref, bs_ref, o_ref, *, n_chunks):
    C = GM_CHUNK
    vn = _layer_norm(v_ref[...].astype(F32), g_ref[...], b_ref[...]).astype(BF16)
    row = lax.broadcasted_iota(jnp.int32, (C, C), 0)
    col = lax.broadcasted_iota(jnp.int32, (C, C), 1)
    for grp in range(ws_ref.shape[0]):
        w = jnp.where(row >= col, ws_ref[grp], 0.0).astype(BF16)
        cols = slice(grp * C, (grp + 1) * C)
        for ch in range(n_chunks):
            rows = slice(ch * C, (ch + 1) * C)
            mixed = _dot(w, vn[rows, cols]) + bs_ref[:, cols]
            o_ref[rows, cols] = (u_ref[rows, cols].astype(F32) * mixed).astype(o_ref.dtype)


def _gm_spatial(z, ln_g, ln_b, w_s, bias_full, *, tm):
    T, W2 = z.shape
    W = W2 // 2
    G, C, _ = w_s.shape
    tm = min(tm, T)
    return pl.pallas_call(
        functools.partial(_gm_spatial_kernel, n_chunks=tm // C),
        grid=(T // tm,),
        in_specs=[pl.BlockSpec((tm, W), lambda i: (i, 0)),
                  pl.BlockSpec((tm, W), lambda i: (i, 1)),
                  _resident((1, W), lambda i: (0, 0)),
                  _resident((1, W), lambda i: (0, 0)),
                  _resident((G, C, C), lambda i: (0, 0, 0)),
                  _resident((C, W), lambda i: (0, 0))],
        out_specs=pl.BlockSpec((tm, W), lambda i: (i, 0)),
        out_shape=jax.ShapeDtypeStruct((T, W), BF16),
        compiler_params=_params(1),
        name="gm_spatial",
    )(z, z, ln_g.reshape(1, W), ln_b.reshape(1, W), w_s, bias_full)


def _router_kernel(x_ref, wrt_ref, o_ref, cnt_ref, carry_ref, *, tm, n_exp):
    @pl.when(pl.program_id(0) == 0)
    def _():
        carry_ref[...] = jnp.zeros_like(carry_ref)

    x = x_ref[...]
    xh = x.astype(BF16)
    xl = (x - xh.astype(F32)).astype(BF16)
    wh, wl = wrt_ref[0], wrt_ref[1]
    logits = _dot_nt(wh, xh) + _dot_nt(wh, xl) + _dot_nt(wl, xh)
    row = lax.broadcasted_iota(jnp.int32, logits.shape, 0)
    logits = jnp.where(row < n_exp, logits, NEG)
    top1 = jnp.max(logits, axis=0, keepdims=True)
    idx1 = jnp.min(jnp.where(logits == top1, row, EXPERT_ROWS), axis=0, keepdims=True)
    rest = jnp.where(row == idx1, NEG, logits)
    top2 = jnp.max(rest, axis=0, keepdims=True)
    idx2 = jnp.min(jnp.where(rest == top2, row, EXPERT_ROWS), axis=0, keepdims=True)
    e = jnp.exp(top2 - top1)
    gate1 = 1.0 / (1.0 + e)
    gate2 = e / (1.0 + e)

    sel = jnp.where((row == idx1) | (row == idx2), 1.0, 0.0)
    r = lax.broadcasted_iota(jnp.int32, (tm, tm), 0)
    c = lax.broadcasted_iota(jnp.int32, (tm, tm), 1)
    tri = jnp.where(r <= c, 1.0, 0.0).astype(BF16)
    incl = _dot(sel.astype(BF16), tri)
    excl = incl - sel + carry_ref[...]
    rank1 = jnp.sum(jnp.where(row == idx1, excl, 0.0), axis=0, keepdims=True)
    rank2 = jnp.sum(jnp.where(row == idx2, excl, 0.0), axis=0, keepdims=True)
    total = carry_ref[...] + incl[:, tm - 1:tm]
    carry_ref[...] = total
    cnt_ref[...] = jnp.broadcast_to(total, cnt_ref.shape)
    o_ref[...] = jnp.concatenate(
        [idx1.astype(F32), idx2.astype(F32), gate1, gate2, rank1, rank2,
         jnp.zeros((2, tm), F32)], axis=0)


def _router(xf, w_router, *, tm):
    T, D = xf.shape
    n_exp = w_router.shape[1]
    tm = min(tm, T)
    wt = jnp.zeros((EXPERT_ROWS, D), F32).at[:n_exp].set(w_router.T.astype(F32))
    wh = wt.astype(BF16)
    wl = (wt - wh.astype(F32)).astype(BF16)
    return pl.pallas_call(
        functools.partial(_router_kernel, tm=tm, n_exp=n_exp),
        grid=(T // tm,),
        in_specs=[pl.BlockSpec((tm, D), lambda i: (i, 0)),
                  _resident((2, EXPERT_ROWS, D), lambda i: (0, 0, 0))],
        out_specs=[pl.BlockSpec((8, tm), lambda i: (0, i)),
                   pl.BlockSpec((EXPERT_ROWS, 128), lambda i: (0, 0))],
        out_shape=[jax.ShapeDtypeStruct((8, T), F32),
                   jax.ShapeDtypeStruct((EXPERT_ROWS, 128), F32)],
        scratch_shapes=[pltpu.VMEM((EXPERT_ROWS, 1), F32)],
        compiler_params=_params(1),
        name="router",
    )(xf, jnp.stack([wh, wl]))


def _grouped_kernel(first_ref, count_ref, w_hbm, x_hbm, o_hbm, wbuf, w_bf, xbuf, obuf, w_sem, in_sem, out_sem,
                    *, tm, tn, n_tiles, gated):
    n_w = 2 if gated else 1
    n, e = pl.program_id(0), pl.program_id(1)
    nb, n_exp = pl.num_programs(0), pl.num_programs(1)
    step = n * n_exp + e
    first, count = first_ref[e], count_ref[e]
    col = pl.multiple_of(n * tn, 128)

    def w_copy(s, j):
        cols = pl.ds(pl.multiple_of((j * nb + s // n_exp) * tn, 128), tn)
        return pltpu.make_async_copy(w_hbm.at[s % n_exp, :, cols], wbuf.at[s % 2, j], w_sem.at[s % 2, j])

    def in_copy(tile, slot):
        rows = pl.ds(pl.multiple_of(tile * tm, tm), tm)
        return pltpu.make_async_copy(x_hbm.at[rows], xbuf.at[slot], in_sem.at[slot])

    def out_copy(tile, slot):
        rows = pl.ds(pl.multiple_of(tile * tm, tm), tm)
        return pltpu.make_async_copy(obuf.at[slot], o_hbm.at[rows, pl.ds(col, tn)], out_sem.at[slot])

    @pl.when(step == 0)
    def _():
        for j in range(n_w):
            w_copy(step, j).start(priority=1)

    @pl.when(step + 1 < nb * n_exp)
    def _():
        for j in range(n_w):
            w_copy(step + 1, j).start(priority=1)

    for j in range(n_w):
        w_copy(step, j).wait()

    @pl.when(count > 0)
    def _():
        for j in range(n_w):
            w_bf[j] = wbuf[step % 2, j].astype(BF16)
        in_copy(first, 0).start()

        def body(i, c):
            slot = i % 2
            in_copy(first + i, slot).wait()

            @pl.when(i + 1 < count)
            def _():
                in_copy(first + i + 1, 1 - slot).start()

            @pl.when(i >= 2)
            def _():
                out_copy(first + i - 2, slot).wait()

            x = xbuf[slot]
            if x.dtype != BF16:
                x = x.astype(BF16)
            y = _dot(x, w_bf[0])
            if gated:
                y = y * _sigmoid(y) * _dot(x, w_bf[1])
            obuf[slot] = y.astype(obuf.dtype)
            out_copy(first + i, slot).start()
            return c

        lax.fori_loop(0, count, body, 0)

        @pl.when(count >= 2)
        def _():
            out_copy(first + count - 2, count % 2).wait()

        out_copy(first + count - 1, (count - 1) % 2).wait()

    @pl.when(e == pl.num_programs(1) - 1)
    def _():
        obuf[0] = jnp.zeros(obuf.shape[1:], obuf.dtype)

        def start_zero(t, c):
            out_copy(t, 0).start()
            return c

        def wait_zero(t, c):
            out_copy(t, 0).wait()
            return c

        lax.fori_loop(first + count, n_tiles, start_zero, 0)
        lax.fori_loop(first + count, n_tiles, wait_zero, 0)


def _grouped_mm(xs, w, first_tile, tile_count, *, tm, tn, gated, out_dtype):
    P, K = xs.shape
    E = w.shape[0]
    N = w.shape[2] // 2 if gated else w.shape[2]
    tn = min(tn, N)
    n_w = 2 if gated else 1
    return pl.pallas_call(
        functools.partial(_grouped_kernel, tm=tm, tn=tn, n_tiles=P // tm, gated=gated),
        grid_spec=pltpu.PrefetchScalarGridSpec(
            num_scalar_prefetch=2,
            grid=(N // tn, E),
            in_specs=[pl.BlockSpec(memory_space=pl.ANY), pl.BlockSpec(memory_space=pl.ANY)],
            out_specs=pl.BlockSpec(memory_space=pl.ANY),
            scratch_shapes=[pltpu.VMEM((2, n_w, K, tn), w.dtype), pltpu.VMEM((n_w, K, tn), BF16),
                            pltpu.VMEM((2, tm, K), xs.dtype), pltpu.VMEM((2, tm, tn), out_dtype),
                            pltpu.SemaphoreType.DMA((2, n_w)),
                            pltpu.SemaphoreType.DMA((2,)), pltpu.SemaphoreType.DMA((2,))]),
        out_shape=jax.ShapeDtypeStruct((P, N), out_dtype),
        compiler_params=_params(2),
        name="moe_up" if gated else "moe_down",
    )(first_tile, tile_count, w, xs)


def _dispatch_kernel(pos_ref, pad_ref, x_ref, xs_hbm, zero_ref, sem, *, tokens_per_step, n_tokens, n_groups):
    base = pl.program_id(0) * tokens_per_step

    def token_copy(j, k):
        dst = pos_ref[k * n_tokens + base + j]
        return pltpu.make_async_copy(x_ref.at[pl.ds(j, 1)], xs_hbm.at[pl.ds(dst, 1)], sem)

    def pad_copy(dst):
        return pltpu.make_async_copy(zero_ref, xs_hbm.at[pl.ds(dst, 1)], sem)

    @pl.when(pl.program_id(0) == 0)
    def _():
        zero_ref[...] = jnp.zeros_like(zero_ref)
        for e in range(n_groups):
            first, count = pad_ref[e], pad_ref[n_groups + e]

            def start_pad(j, c):
                pad_copy(first + j).start()
                return c

            def wait_pad(j, c):
                pad_copy(first + j).wait()
                return c

            lax.fori_loop(0, count, start_pad, 0)
            lax.fori_loop(0, count, wait_pad, 0)

    def start_rows(j, c):
        for k in range(TOP_K):
            token_copy(j, k).start()
        return c

    def wait_rows(j, c):
        for k in range(TOP_K):
            token_copy(j, k).wait()
        return c

    lax.fori_loop(0, tokens_per_step, start_rows, 0, unroll=16)
    lax.fori_loop(0, tokens_per_step, wait_rows, 0, unroll=16)


def _dispatch(xf, pos_flat, pad_info, *, n_rows, tokens_per_step):
    T, D = xf.shape
    tokens_per_step = min(tokens_per_step, T)
    return pl.pallas_call(
        functools.partial(_dispatch_kernel, tokens_per_step=tokens_per_step, n_tokens=T,
                          n_groups=pad_info.shape[0] // 2),
        grid_spec=pltpu.PrefetchScalarGridSpec(
            num_scalar_prefetch=2,
            grid=(T // tokens_per_step,),
            in_specs=[pl.BlockSpec((tokens_per_step, D), lambda i, pos, pad: (i, 0))],
            out_specs=pl.BlockSpec(memory_space=pl.ANY),
            scratch_shapes=[pltpu.VMEM((1, D), F32), pltpu.SemaphoreType.DMA(())]),
        out_shape=jax.ShapeDtypeStruct((n_rows, D), F32),
        compiler_params=_params(1),
        name="moe_dispatch",
    )(pos_flat, pad_info, xf)


def _combine_ln_kernel(pos_ref, r_ref, g1_ref, g2_ref, g_ref, b_ref, ys_hbm, of_ref, ob_ref, buf, sem,
                       *, tm, n_tokens):
    i = pl.program_id(0)

    def row_copy(tile, slot, j, k):
        src = pos_ref[k * n_tokens + tile * tm + j]
        return pltpu.make_async_copy(ys_hbm.at[pl.ds(src, 1)], buf.at[slot, k, pl.ds(j, 1)], sem.at[slot])

    def fetch(tile, slot):
        for j in range(tm):
            for k in range(TOP_K):
                row_copy(tile, slot, j, k).start()

    def drain(tile, slot):
        for j in range(tm):
            for k in range(TOP_K):
                row_copy(tile, slot, j, k).wait()

    @pl.when(i == 0)
    def _():
        fetch(0, 0)

    last = pl.num_programs(0) - 1
    slot = i % 2
    fetch(jnp.minimum(i + 1, last), 1 - slot)
    drain(i, slot)
    ch = g1_ref[...] * buf[slot, 0] + g2_ref[...] * buf[slot, 1]
    y = _layer_norm(ALPHA * r_ref[...] + ch, g_ref[...], b_ref[...])
    of_ref[...] = y
    ob_ref[...] = y.astype(BF16)

    @pl.when(i == last)
    def _():
        drain(last, 1 - slot)


def _combine_ln(res, ys, pos_flat, g1, g2, g, b, *, tm):
    M, D = res.shape
    tm = min(tm, M)
    tile = pl.BlockSpec((tm, D), lambda m, pos: (m, 0))
    col = pl.BlockSpec((tm, 1), lambda m, pos: (m, 0))
    vec = _resident((1, D), lambda m, pos: (0, 0))
    return pl.pallas_call(
        functools.partial(_combine_ln_kernel, tm=tm, n_tokens=M),
        grid_spec=pltpu.PrefetchScalarGridSpec(
            num_scalar_prefetch=1,
            grid=(M // tm,),
            in_specs=[tile, col, col, vec, vec, pl.BlockSpec(memory_space=pl.ANY)],
            out_specs=[tile, tile],
            scratch_shapes=[pltpu.VMEM((2, TOP_K, tm, D), F32), pltpu.SemaphoreType.DMA((2,))]),
        out_shape=[jax.ShapeDtypeStruct((M, D), F32), jax.ShapeDtypeStruct((M, D), BF16)],
        compiler_params=_params(1),
        name="moe_combine_ln",
    )(pos_flat, res, g1, g2, g.reshape(1, D), b.reshape(1, D), ys)


def _moe_layer(xf, w_router, w_gu, w_down, ln_g, ln_b):
    T, D = xf.shape
    n_exp = w_router.shape[1]
    tm = min(MOE_TM, T)
    n_tiles = (T * TOP_K) // tm + n_exp

    route, counts = _router(xf, w_router, tm=512)
    idx = route[0:2].astype(jnp.int32)
    gates = route[2:4]
    rank = route[4:6].astype(jnp.int32)
    counts = counts[:n_exp, 0].astype(jnp.int32)
    tiles_per_expert = (counts + tm - 1) // tm
    tile_end = jnp.cumsum(tiles_per_expert)
    row_start = (tile_end - tiles_per_expert) * tm
    experts = jnp.arange(n_exp, dtype=jnp.int32)[:, None, None]
    pos = jnp.sum(jnp.where(idx[None] == experts, row_start[:, None, None], 0), axis=0) + rank
    pos_flat = pos.reshape(-1)
    pad_first = jnp.concatenate([row_start + counts, tile_end[-1:] * tm])
    pad_count = jnp.concatenate([tiles_per_expert * tm - counts, (n_tiles - tile_end[-1:]) * tm])
    pad_info = jnp.concatenate([pad_first, pad_count]).astype(jnp.int32)
    first_tile = (tile_end - tiles_per_expert).astype(jnp.int32)
    tile_count = tiles_per_expert.astype(jnp.int32)

    xs = _dispatch(xf, pos_flat, pad_info, n_rows=n_tiles * tm, tokens_per_step=512)
    hs = _grouped_mm(xs, w_gu, first_tile, tile_count, tm=tm, tn=896, gated=True, out_dtype=BF16)
    ys = _grouped_mm(hs, w_down, first_tile, tile_count, tm=tm, tn=512, gated=False, out_dtype=F32)
    return _combine_ln(xf, ys, pos_flat, gates[0].reshape(T, 1), gates[1].reshape(T, 1), ln_g, ln_b, tm=256)


def kernel(x, p, fox_w_in, fox_b_f, fox_w_o, gm_w_in, gm_ln_v_g, gm_ln_v_b, gm_w_s, gm_b_s, gm_w_o, ffn_w_gu, ffn_w_down, moe_w_router, moe_w_gu, moe_w_down, ln_mix_g, ln_mix_b, ln_ch_g, ln_ch_b, ple_w_proj, ple_w_gate):
    B, S, D = x.shape
    T = B * S
    H = D // HEAD_DIM
    xf = x.reshape(T, D)
    xb = xf.astype(BF16)
    pb = p.reshape(p.shape[0], T, p.shape[-1]).astype(BF16)

    w_in = fox_w_in[0]
    qkv = _mm(xb, w_in[:, :3 * D].astype(BF16), tm=1024, tn=1024, out_dtype=BF16)
    c = _fox_gate(xb, w_in[:, 3 * D:].T.astype(BF16), fox_b_f[0], batch=B, ts=512)
    attn = _fox_attn(qkv, c, batch=B, tq=1024, n_sub=4)
    xf, xb = _mm_res_ln(attn, fox_w_o[0].astype(BF16), xf, ln_mix_g[0], ln_mix_b[0], tm=256)
    hid = _swiglu_up(xb, ffn_w_gu[0].astype(BF16), tm=1024, tn=512)
    xf, xb = _mm_res_ln(hid, ffn_w_down[0].astype(BF16), xf, ln_ch_g[0], ln_ch_b[0], tm=256)
    xf, xb = _ple(xb, xf, ple_w_gate[0].astype(BF16), pb[0], ple_w_proj[0].astype(BF16), tm=512, tn=1024)

    z = _mm(xb, gm_w_in[0].astype(BF16), tm=1024, tn=1024, out_dtype=BF16, act="gelu")
    bias_full = jnp.repeat(gm_b_s[0].T.astype(F32), D // gm_b_s.shape[1], axis=1)
    y = _gm_spatial(z, gm_ln_v_g[0], gm_ln_v_b[0], gm_w_s[0], bias_full, tm=512)
    xf, xb = _mm_res_ln(y, gm_w_o[0].astype(BF16), xf, ln_mix_g[1], ln_mix_b[1], tm=256)
    xf, xb = _moe_layer(xf, moe_w_router[0], moe_w_gu[0], moe_w_down[0],
                        ln_ch_g[1], ln_ch_b[1])
    xf, _ = _ple(xb, xf, ple_w_gate[1].astype(BF16), pb[1], ple_w_proj[1].astype(BF16), tm=512, tn=1024)
    return xf.reshape(B, S, D)
```

```python
import functools

import jax
import jax.numpy as jnp
from jax import lax
from jax.experimental import pallas as pl
from jax.experimental.pallas import tpu as pltpu

F32 = jnp.float32
BF16 = jnp.bfloat16

LN_EPS = 1e-5
DEPTH = 2
ALPHA = (2.0 * DEPTH) ** 0.25
HEAD_DIM = 128
GM_CHUNK = 128
GM_GROUPS = 16
TOP_K = 2
NEG = -1e30
V7X_VMEM_LIMIT_BYTES = 56 * 2**20
EXPERT_ROWS = 16
MOE_TM = 256


def _params(n_axes):
    return pltpu.CompilerParams(dimension_semantics=("arbitrary",) * n_axes,
                                vmem_limit_bytes=V7X_VMEM_LIMIT_BYTES)


def _dot(a, b):
    return jnp.dot(a, b, preferred_element_type=F32)


def _dot_nt(a, b):
    return lax.dot_general(a, b, (((1,), (1,)), ((), ())), preferred_element_type=F32)


def _sigmoid(x):
    return 1.0 / (1.0 + jnp.exp(-x))


def _gelu_tanh(x):
    return 0.5 * x * (1.0 + jnp.tanh(0.7978845608028654 * (x + 0.044715 * (x * x * x))))


def _layer_norm(y, g, b):
    mu = jnp.mean(y, axis=-1, keepdims=True)
    d = y - mu
    var = jnp.mean(d * d, axis=-1, keepdims=True)
    return d * lax.rsqrt(var + LN_EPS) * g + b


def _resident(block_shape, index_map):
    return pl.BlockSpec(block_shape, index_map, pipeline_mode=pl.Buffered(1))


def _bf16_weight(w_ref, cache, fresh):
    if w_ref.dtype == BF16:
        return w_ref[...]
    (cache_ref,) = cache

    @pl.when(fresh)
    def _():
        cache_ref[...] = w_ref[...].astype(BF16)

    return cache_ref[...]


def _weight_cache(w, block_shape):
    return [] if w.dtype == BF16 else [pltpu.VMEM(block_shape, BF16)]


def _mm_kernel(a_ref, w_ref, o_ref, *cache, act):
    y = _dot(a_ref[...], _bf16_weight(w_ref, cache, pl.program_id(1) == 0))
    if act == "gelu":
        y = _gelu_tanh(y)
    o_ref[...] = y.astype(o_ref.dtype)


def _mm(a, w, *, n_cols, tm, tn, out_dtype, act=None):
    M, K = a.shape
    tm, tn = min(tm, M), min(tn, n_cols)
    return pl.pallas_call(
        functools.partial(_mm_kernel, act=act),
        grid=(n_cols // tn, M // tm),
        in_specs=[pl.BlockSpec((tm, K), lambda n, m: (m, 0)),
                  pl.BlockSpec((K, tn), lambda n, m: (0, n))],
        out_specs=pl.BlockSpec((tm, tn), lambda n, m: (m, n)),
        out_shape=jax.ShapeDtypeStruct((M, n_cols), out_dtype),
        scratch_shapes=_weight_cache(w, (K, tn)),
        compiler_params=_params(2),
        name="mm_" + (act or "plain"),
    )(a, w)


def _mm_res_ln_kernel(a_ref, w_ref, r_ref, g_ref, b_ref, of_ref, ob_ref, *cache):
    y = ALPHA * r_ref[...] + _dot(a_ref[...], _bf16_weight(w_ref, cache, pl.program_id(0) == 0))
    y = _layer_norm(y, g_ref[...], b_ref[...])
    of_ref[...] = y
    ob_ref[...] = y.astype(BF16)


def _mm_res_ln(a, w, res, g, b, *, tm):
    M, K = a.shape
    D = w.shape[1]
    tm = min(tm, M)
    return pl.pallas_call(
        _mm_res_ln_kernel,
        grid=(M // tm,),
        in_specs=[pl.BlockSpec((tm, K), lambda m: (m, 0)),
                  _resident((K, D), lambda m: (0, 0)),
                  pl.BlockSpec((tm, D), lambda m: (m, 0)),
                  _resident((1, D), lambda m: (0, 0)),
                  _resident((1, D), lambda m: (0, 0))],
        out_specs=[pl.BlockSpec((tm, D), lambda m: (m, 0)),
                   pl.BlockSpec((tm, D), lambda m: (m, 0))],
        out_shape=[jax.ShapeDtypeStruct((M, D), F32), jax.ShapeDtypeStruct((M, D), BF16)],
        scratch_shapes=_weight_cache(w, (K, D)),
        compiler_params=_params(1),
        name="mm_res_ln",
    )(a, w, res, g.reshape(1, D), b.reshape(1, D))


def _swiglu_up_kernel(x_ref, wg_ref, wu_ref, o_ref, *cache):
    x = x_ref[...]
    fresh = pl.program_id(1) == 0
    g = _dot(x, _bf16_weight(wg_ref, cache[:1], fresh))
    u = _dot(x, _bf16_weight(wu_ref, cache[1:], fresh))
    o_ref[...] = (g * _sigmoid(g) * u).astype(o_ref.dtype)


def _swiglu_up(x, w_gu, *, tm, tn):
    M, K = x.shape
    F = w_gu.shape[1] // 2
    tm, tn = min(tm, M), min(tn, F)
    nf = F // tn
    return pl.pallas_call(
        _swiglu_up_kernel,
        grid=(nf, M // tm),
        in_specs=[pl.BlockSpec((tm, K), lambda n, m: (m, 0)),
                  pl.BlockSpec((K, tn), lambda n, m: (0, n)),
                  pl.BlockSpec((K, tn), lambda n, m: (0, nf + n))],
        out_specs=pl.BlockSpec((tm, tn), lambda n, m: (m, n)),
        out_shape=jax.ShapeDtypeStruct((M, F), BF16),
        scratch_shapes=_weight_cache(w_gu, (K, tn)) * 2,
        compiler_params=_params(2),
        name="swiglu_up",
    )(x, w_gu, w_gu)


def _ple_kernel(xb_ref, wg_ref, p_ref, wp_ref, xf_ref, of_ref, ob_ref, *cache):
    fresh = pl.program_id(1) == 0
    gate = _sigmoid(_dot(xb_ref[...], _bf16_weight(wg_ref, cache[:1], fresh)))
    y = xf_ref[...] + gate * _dot(p_ref[...].astype(BF16), _bf16_weight(wp_ref, cache[1:], fresh))
    of_ref[...] = y
    ob_ref[...] = y.astype(BF16)


def _ple(xb, xf, w_gate, p, w_proj, *, tm, tn):
    M, D = xf.shape
    P = p.shape[1]
    tm, tn = min(tm, M), min(tn, D)
    return pl.pallas_call(
        _ple_kernel,
        grid=(D // tn, M // tm),
        in_specs=[pl.BlockSpec((tm, D), lambda n, m: (m, 0)),
                  pl.BlockSpec((D, tn), lambda n, m: (0, n)),
                  pl.BlockSpec((tm, P), lambda n, m: (m, 0)),
                  pl.BlockSpec((P, tn), lambda n, m: (0, n)),
                  pl.BlockSpec((tm, tn), lambda n, m: (m, n))],
        out_specs=[pl.BlockSpec((tm, tn), lambda n, m: (m, n)),
                   pl.BlockSpec((tm, tn), lambda n, m: (m, n))],
        out_shape=[jax.ShapeDtypeStruct((M, D), F32), jax.ShapeDtypeStruct((M, D), BF16)],
        scratch_shapes=_weight_cache(w_gate, (D, tn)) + _weight_cache(w_proj, (P, tn)),
        compiler_params=_params(2),
        name="ple",
    )(xb, w_gate, p, w_proj, xf)


def _split3_bf16(x):
    hi = x.astype(BF16)
    r = x - hi.astype(F32)
    mid = r.astype(BF16)
    lo = (r - mid.astype(F32)).astype(BF16)
    return hi, mid, lo


def _fox_gate_kernel(x_ref, wft_ref, bf_ref, c_ref, carry_ref, *, ts):
    @pl.when(pl.program_id(1) == 0)
    def _():
        carry_ref[...] = jnp.zeros_like(carry_ref)

    f = _dot_nt(wft_ref[...], x_ref[...]) + bf_ref[...]
    logf = jnp.minimum(f, 0.0) - jnp.log1p(jnp.exp(-jnp.abs(f)))
    row = lax.broadcasted_iota(jnp.int32, (ts, ts), 0)
    col = lax.broadcasted_iota(jnp.int32, (ts, ts), 1)
    tri = jnp.where(row <= col, 1.0, 0.0).astype(BF16)
    hi, mid, lo = _split3_bf16(logf)
    c = _dot(hi, tri) + _dot(mid, tri) + _dot(lo, tri) + carry_ref[...]
    c_ref[0] = c
    carry_ref[...] = c[:, ts - 1:ts]


def _fox_gate(xb, wft, b_f, *, batch, ts):
    T, D = xb.shape
    H = wft.shape[0]
    S = T // batch
    ts = min(ts, S)
    ns = S // ts
    return pl.pallas_call(
        functools.partial(_fox_gate_kernel, ts=ts),
        grid=(batch, ns),
        in_specs=[pl.BlockSpec((ts, D), lambda b, i: (b * ns + i, 0)),
                  _resident((H, D), lambda b, i: (0, 0)),
                  _resident((H, 1), lambda b, i: (0, 0))],
        out_specs=pl.BlockSpec((1, H, ts), lambda b, i: (b, 0, i)),
        out_shape=jax.ShapeDtypeStruct((batch, H, S), F32),
        scratch_shapes=[pltpu.VMEM((H, 1), F32)],
        compiler_params=_params(2),
        name="fox_gate",
    )(xb, wft, b_f.reshape(H, 1).astype(F32))


def _fox_attn_kernel(q_ref, k_ref, v_ref, c_ref, o_ref, kaug, vaug, *, tq, n_sub, scale):
    qi = pl.program_id(2)
    sub = tq // n_sub
    hd = HEAD_DIM
    log2e = 1.4426950408889634
    S = k_ref.shape[0]

    @pl.when(qi == 0)
    def _():
        lane = lax.broadcasted_iota(jnp.int32, (S, hd), 1)
        c_col = jnp.transpose(jnp.broadcast_to(c_ref[0] * (-log2e), (8, S)))[:, 0:1]
        hi, mid, lo = _split3_bf16(c_col)
        extra = jnp.where(lane == 0, hi.astype(F32), jnp.where(lane == 1, mid.astype(F32),
                          jnp.where(lane == 2, lo.astype(F32), 0.0)))
        kaug[:, :hd] = k_ref[...]
        kaug[:, hd:] = extra.astype(BF16)
        vaug[:, :hd] = v_ref[...]
        vaug[:, hd:] = jnp.where(lane == 0, 1.0, 0.0).astype(BF16)

    q_lane = lax.broadcasted_iota(jnp.int32, (sub, hd), 1)
    q_extra = jnp.where(q_lane < 3, 1.0, 0.0).astype(BF16)
    qs = [jnp.concatenate([(q_ref[r * sub:(r + 1) * sub, :].astype(F32) * (scale * log2e)).astype(BF16), q_extra],
                          axis=1) for r in range(n_sub)]

    def update(carry, q, start, width, row_offset):
        m, acc = carry
        s = _dot_nt(q, kaug[pl.ds(start, width), :])
        if row_offset is not None:
            row = lax.broadcasted_iota(jnp.int32, s.shape, 0) + row_offset
            col = lax.broadcasted_iota(jnp.int32, s.shape, 1)
            s = jnp.where(row >= col, s, NEG)
        m_new = jnp.maximum(m, jnp.max(s, axis=-1, keepdims=True))
        p = jnp.exp2(s - m_new).astype(BF16)
        acc = jnp.exp2(m - m_new) * acc + _dot(p, vaug[pl.ds(start, width), :])
        return m_new, acc

    def full_block(j, carries):
        start = pl.multiple_of(j * tq, tq)
        return tuple(update(carries[r], qs[r], start, tq, None) for r in range(n_sub))

    init = tuple((jnp.full((sub, 1), NEG, F32), jnp.zeros((sub, 2 * hd), F32)) for _ in range(n_sub))
    carries = lax.fori_loop(0, qi, full_block, init)

    start = pl.multiple_of(qi * tq, tq)
    for r in range(n_sub):
        _, acc = update(carries[r], qs[r], start, (r + 1) * sub, r * sub)
        o_ref[r * sub:(r + 1) * sub, :] = (acc[:, :hd] / acc[:, hd:hd + 1]).astype(o_ref.dtype)


def _fox_attn(qkv, c, *, batch, tq, n_sub):
    T, D3 = qkv.shape
    D = D3 // 3
    H = D // HEAD_DIM
    S = T // batch
    tq = min(tq, S)
    nq = S // tq
    c3 = c.reshape(batch * H, 1, S)
    return pl.pallas_call(
        functools.partial(_fox_attn_kernel, tq=tq, n_sub=n_sub, scale=HEAD_DIM ** -0.5),
        grid=(batch, H, nq),
        in_specs=[pl.BlockSpec((tq, HEAD_DIM), lambda b, h, i: (b * nq + i, h)),
                  pl.BlockSpec((S, HEAD_DIM), lambda b, h, i: (b, H + h)),
                  pl.BlockSpec((S, HEAD_DIM), lambda b, h, i: (b, 2 * H + h)),
                  pl.BlockSpec((1, 1, S), lambda b, h, i: (b * H + h, 0, 0))],
        out_specs=pl.BlockSpec((tq, HEAD_DIM), lambda b, h, i: (b * nq + i, h)),
        out_shape=jax.ShapeDtypeStruct((T, D), BF16),
        scratch_shapes=[pltpu.VMEM((S, 2 * HEAD_DIM), BF16), pltpu.VMEM((S, 2 * HEAD_DIM), BF16)],
        compiler_params=_params(3),
        name="fox_attn",
    )(qkv, qkv, qkv, c3)


def _gm_spatial_kernel(u_ref, v_ref, g_ref, b_ref, ws_ref, bs_ref, o_ref, *, n_chunks):
    C = GM_CHUNK
    vn = _layer_norm(v_ref[...].astype(F32), g_ref[...], b_ref[...]).astype(BF16)
    row = lax.broadcasted_iota(jnp.int32, (C, C), 0)
    col = lax.broadcasted_iota(jnp.int32, (C, C), 1)
    for grp in range(ws_ref.shape[0]):
        w = jnp.where(row >= col, ws_ref[grp], 0.0).astype(BF16)
        cols = slice(grp * C, (grp + 1) * C)
        for ch in range(n_chunks):
            rows = slice(ch * C, (ch + 1) * C)
            mixed = _dot(w, vn[rows, cols]) + bs_ref[:, cols]
            o_ref[rows, cols] = (u_ref[rows, cols].astype(F32) * mixed).astype(o_ref.dtype)


def _gm_spatial(z, ln_g, ln_b, w_s, bias_full, *, tm):
    T, W2 = z.shape
    W = W2 // 2
    G, C, _ = w_s.shape
    tm = min(tm, T)
    return pl.pallas_call(
        functools.partial(_gm_spatial_kernel, n_chunks=tm // C),
        grid=(T // tm,),
        in_specs=[pl.BlockSpec((tm, W), lambda i: (i, 0)),
                  pl.BlockSpec((tm, W), lambda i: (i, 1)),
                  _resident((1, W), lambda i: (0, 0)),
                  _resident((1, W), lambda i: (0, 0)),
                  _resident((G, C, C), lambda i: (0, 0, 0)),
                  _resident((C, W), lambda i: (0, 0))],
        out_specs=pl.BlockSpec((tm, W), lambda i: (i, 0)),
        out_shape=jax.ShapeDtypeStruct((T, W), BF16),
        compiler_params=_params(1),
        name="gm_spatial",
    )(z, z, ln_g.reshape(1, W), ln_b.reshape(1, W), w_s, bias_full)


def _router_kernel(x_ref, wrt_ref, o_ref, cnt_ref, carry_ref, *, tm, n_exp):
    @pl.when(pl.program_id(0) == 0)
    def _():
        carry_ref[...] = jnp.zeros_like(carry_ref)

    x = x_ref[...]
    xh = x.astype(BF16)
    xl = (x - xh.astype(F32)).astype(BF16)
    wh, wl = wrt_ref[0], wrt_ref[1]
    logits = _dot_nt(wh, xh) + _dot_nt(wh, xl) + _dot_nt(wl, xh)
    row = lax.broadcasted_iota(jnp.int32, logits.shape, 0)
    logits = jnp.where(row < n_exp, logits, NEG)
    top1 = jnp.max(logits, axis=0, keepdims=True)
    idx1 = jnp.min(jnp.where(logits == top1, row, EXPERT_ROWS), axis=0, keepdims=True)
    rest = jnp.where(row == idx1, NEG, logits)
    top2 = jnp.max(rest, axis=0, keepdims=True)
    idx2 = jnp.min(jnp.where(rest == top2, row, EXPERT_ROWS), axis=0, keepdims=True)
    e = jnp.exp(top2 - top1)
    gate1 = 1.0 / (1.0 + e)
    gate2 = e / (1.0 + e)

    sel = jnp.where((row == idx1) | (row == idx2), 1.0, 0.0)
    r = lax.broadcasted_iota(jnp.int32, (tm, tm), 0)
    c = lax.broadcasted_iota(jnp.int32, (tm, tm), 1)
    tri = jnp.where(r <= c, 1.0, 0.0).astype(BF16)
    incl = _dot(sel.astype(BF16), tri)
    excl = incl - sel + carry_ref[...]
    rank1 = jnp.sum(jnp.where(row == idx1, excl, 0.0), axis=0, keepdims=True)
    rank2 = jnp.sum(jnp.where(row == idx2, excl, 0.0), axis=0, keepdims=True)
    total = carry_ref[...] + incl[:, tm - 1:tm]
    carry_ref[...] = total
    cnt_ref[...] = jnp.broadcast_to(total, cnt_ref.shape)
    o_ref[...] = jnp.concatenate(
        [idx1.astype(F32), idx2.astype(F32), gate1, gate2, rank1, rank2,
         jnp.zeros((2, tm), F32)], axis=0)


def _router(xf, w_router, *, tm):
    T, D = xf.shape
    n_exp = w_router.shape[1]
    tm = min(tm, T)
    wt = jnp.zeros((EXPERT_ROWS, D), F32).at[:n_exp].set(w_router.T.astype(F32))
    wh = wt.astype(BF16)
    wl = (wt - wh.astype(F32)).astype(BF16)
    return pl.pallas_call(
        functools.partial(_router_kernel, tm=tm, n_exp=n_exp),
        grid=(T // tm,),
        in_specs=[pl.BlockSpec((tm, D), lambda i: (i, 0)),
                  _resident((2, EXPERT_ROWS, D), lambda i: (0, 0, 0))],
        out_specs=[pl.BlockSpec((8, tm), lambda i: (0, i)),
                   pl.BlockSpec((EXPERT_ROWS, 128), lambda i: (0, 0))],
        out_shape=[jax.ShapeDtypeStruct((8, T), F32),
                   jax.ShapeDtypeStruct((EXPERT_ROWS, 128), F32)],
        scratch_shapes=[pltpu.VMEM((EXPERT_ROWS, 1), F32)],
        compiler_params=_params(1),
        name="router",
    )(xf, jnp.stack([wh, wl]))


def _grouped_kernel(first_ref, count_ref, w_hbm, x_hbm, o_hbm, wbuf, w_bf, xbuf, obuf, w_sem, in_sem, out_sem,
                    *, tm, tn, n_tiles, gated):
    n_w = 2 if gated else 1
    n, e = pl.program_id(0), pl.program_id(1)
    nb, n_exp = pl.num_programs(0), pl.num_programs(1)
    step = n * n_exp + e
    first, count = first_ref[e], count_ref[e]
    col = pl.multiple_of(n * tn, 128)

    def w_copy(s, j):
        cols = pl.ds(pl.multiple_of((j * nb + s // n_exp) * tn, 128), tn)
        return pltpu.make_async_copy(w_hbm.at[s % n_exp, :, cols], wbuf.at[s % 2, j], w_sem.at[s % 2, j])

    def in_copy(tile, slot):
        rows = pl.ds(pl.multiple_of(tile * tm, tm), tm)
        return pltpu.make_async_copy(x_hbm.at[rows], xbuf.at[slot], in_sem.at[slot])

    def out_copy(tile, slot):
        rows = pl.ds(pl.multiple_of(tile * tm, tm), tm)
        return pltpu.make_async_copy(obuf.at[slot], o_hbm.at[rows, pl.ds(col, tn)], out_sem.at[slot])

    @pl.when(step == 0)
    def _():
        for j in range(n_w):
            w_copy(step, j).start(priority=1)

    @pl.when(step + 1 < nb * n_exp)
    def _():
        for j in range(n_w):
            w_copy(step + 1, j).start(priority=1)

    for j in range(n_w):
        w_copy(step, j).wait()

    @pl.when(count > 0)
    def _():
        @pl.when(step == 0)
        def _():
            in_copy(first, 0).start()

        for j in range(n_w):
            w_bf[j] = wbuf[step % 2, j].astype(BF16)

        def body(i, c):
            slot = i % 2
            in_copy(first + i, slot).wait()

            @pl.when(i + 1 < count)
            def _():
                in_copy(first + i + 1, 1 - slot).start()

            @pl.when(i >= 2)
            def _():
                out_copy(first + i - 2, slot).wait()

            x = xbuf[slot]
            if x.dtype != BF16:
                x = x.astype(BF16)
            y = _dot(x, w_bf[0])
            if gated:
                y = y * _sigmoid(y) * _dot(x, w_bf[1])
            obuf[slot] = y.astype(obuf.dtype)
            out_copy(first + i, slot).start()
            return c

        lax.fori_loop(0, count, body, 0)

        @pl.when(count >= 2)
        def _():
            out_copy(first + count - 2, count % 2).wait()

        out_copy(first + count - 1, (count - 1) % 2).wait()

    next_e = (step + 1) % n_exp

    @pl.when(jnp.logical_and(step + 1 < nb * n_exp, count_ref[next_e] > 0))
    def _():
        in_copy(first_ref[next_e], 0).start()

    @pl.when(e == pl.num_programs(1) - 1)
    def _():
        obuf[0] = jnp.zeros(obuf.shape[1:], obuf.dtype)

        def start_zero(t, c):
            out_copy(t, 0).start()
            return c

        def wait_zero(t, c):
            out_copy(t, 0).wait()
            return c

        lax.fori_loop(first + count, n_tiles, start_zero, 0)
        lax.fori_loop(first + count, n_tiles, wait_zero, 0)


def _grouped_mm(xs, w, first_tile, tile_count, *, tm, tn, gated, out_dtype):
    P, K = xs.shape
    E = w.shape[0]
    N = w.shape[2] // 2 if gated else w.shape[2]
    tn = min(tn, N)
    n_w = 2 if gated else 1
    return pl.pallas_call(
        functools.partial(_grouped_kernel, tm=tm, tn=tn, n_tiles=P // tm, gated=gated),
        grid_spec=pltpu.PrefetchScalarGridSpec(
            num_scalar_prefetch=2,
            grid=(N // tn, E),
            in_specs=[pl.BlockSpec(memory_space=pl.ANY), pl.BlockSpec(memory_space=pl.ANY)],
            out_specs=pl.BlockSpec(memory_space=pl.ANY),
            scratch_shapes=[pltpu.VMEM((2, n_w, K, tn), w.dtype), pltpu.VMEM((n_w, K, tn), BF16),
                            pltpu.VMEM((2, tm, K), xs.dtype), pltpu.VMEM((2, tm, tn), out_dtype),
                            pltpu.SemaphoreType.DMA((2, n_w)),
                            pltpu.SemaphoreType.DMA((2,)), pltpu.SemaphoreType.DMA((2,))]),
        out_shape=jax.ShapeDtypeStruct((P, N), out_dtype),
        compiler_params=_params(2),
        name="moe_up" if gated else "moe_down",
    )(first_tile, tile_count, w, xs)


def _dispatch_kernel(pos_ref, pad_ref, x_ref, xs_hbm, zero_ref, sem, *, tokens_per_step, n_tokens, n_groups):
    base = pl.program_id(0) * tokens_per_step

    def token_copy(j, k):
        dst = pos_ref[k * n_tokens + base + j]
        return pltpu.make_async_copy(x_ref.at[pl.ds(j, 1)], xs_hbm.at[pl.ds(dst, 1)], sem)

    def pad_copy(dst):
        return pltpu.make_async_copy(zero_ref, xs_hbm.at[pl.ds(dst, 1)], sem)

    @pl.when(pl.program_id(0) == 0)
    def _():
        zero_ref[...] = jnp.zeros_like(zero_ref)
        for e in range(n_groups):
            first, count = pad_ref[e], pad_ref[n_groups + e]

            def start_pad(j, c):
                pad_copy(first + j).start()
                return c

            def wait_pad(j, c):
                pad_copy(first + j).wait()
                return c

            lax.fori_loop(0, count, start_pad, 0)
            lax.fori_loop(0, count, wait_pad, 0)

    def start_rows(j, c):
        for k in range(TOP_K):
            token_copy(j, k).start()
        return c

    def wait_rows(j, c):
        for k in range(TOP_K):
            token_copy(j, k).wait()
        return c

    lax.fori_loop(0, tokens_per_step, start_rows, 0, unroll=16)
    lax.fori_loop(0, tokens_per_step, wait_rows, 0, unroll=16)


def _dispatch(xf, pos_flat, pad_info, *, n_rows, tokens_per_step):
    T, D = xf.shape
    tokens_per_step = min(tokens_per_step, T)
    return pl.pallas_call(
        functools.partial(_dispatch_kernel, tokens_per_step=tokens_per_step, n_tokens=T,
                          n_groups=pad_info.shape[0] // 2),
        grid_spec=pltpu.PrefetchScalarGridSpec(
            num_scalar_prefetch=2,
            grid=(T // tokens_per_step,),
            in_specs=[pl.BlockSpec((tokens_per_step, D), lambda i, pos, pad: (i, 0))],
            out_specs=pl.BlockSpec(memory_space=pl.ANY),
            scratch_shapes=[pltpu.VMEM((1, D), F32), pltpu.SemaphoreType.DMA(())]),
        out_shape=jax.ShapeDtypeStruct((n_rows, D), F32),
        compiler_params=_params(1),
        name="moe_dispatch",
    )(pos_flat, pad_info, xf)


def _combine_ln_kernel(pos_ref, r_ref, g1_ref, g2_ref, g_ref, b_ref, ys_hbm, of_ref, ob_ref, buf, sem,
                       *, tm, n_tokens):
    i = pl.program_id(0)

    def row_copy(tile, slot, j, k):
        src = pos_ref[k * n_tokens + tile * tm + j]
        return pltpu.make_async_copy(ys_hbm.at[pl.ds(src, 1)], buf.at[slot, k, pl.ds(j, 1)], sem.at[slot])

    def fetch(tile, slot):
        for j in range(tm):
            for k in range(TOP_K):
                row_copy(tile, slot, j, k).start()

    def drain(tile, slot):
        for j in range(tm):
            for k in range(TOP_K):
                row_copy(tile, slot, j, k).wait()

    @pl.when(i == 0)
    def _():
        fetch(0, 0)

    last = pl.num_programs(0) - 1
    slot = i % 2
    fetch(jnp.minimum(i + 1, last), 1 - slot)
    drain(i, slot)
    ch = g1_ref[...] * buf[slot, 0] + g2_ref[...] * buf[slot, 1]
    y = _layer_norm(ALPHA * r_ref[...] + ch, g_ref[...], b_ref[...])
    of_ref[...] = y
    ob_ref[...] = y.astype(BF16)

    @pl.when(i == last)
    def _():
        drain(last, 1 - slot)


def _combine_ln(res, ys, pos_flat, g1, g2, g, b, *, tm):
    M, D = res.shape
    tm = min(tm, M)
    tile = pl.BlockSpec((tm, D), lambda m, pos: (m, 0))
    col = pl.BlockSpec((tm, 1), lambda m, pos: (m, 0))
    vec = _resident((1, D), lambda m, pos: (0, 0))
    return pl.pallas_call(
        functools.partial(_combine_ln_kernel, tm=tm, n_tokens=M),
        grid_spec=pltpu.PrefetchScalarGridSpec(
            num_scalar_prefetch=1,
            grid=(M // tm,),
            in_specs=[tile, col, col, vec, vec, pl.BlockSpec(memory_space=pl.ANY)],
            out_specs=[tile, tile],
            scratch_shapes=[pltpu.VMEM((2, TOP_K, tm, D), F32), pltpu.SemaphoreType.DMA((2,))]),
        out_shape=[jax.ShapeDtypeStruct((M, D), F32), jax.ShapeDtypeStruct((M, D), BF16)],
        compiler_params=_params(1),
        name="moe_combine_ln",
    )(pos_flat, res, g1, g2, g.reshape(1, D), b.reshape(1, D), ys)


def _moe_layer(xf, w_router, w_gu, w_down, ln_g, ln_b):
    T, D = xf.shape
    n_exp = w_router.shape[1]
    tm = min(MOE_TM, T)
    n_tiles = (T * TOP_K) // tm + n_exp

    route, counts = _router(xf, w_router, tm=512)
    idx = route[0:2].astype(jnp.int32)
    gates = route[2:4]
    rank = route[4:6].astype(jnp.int32)
    counts = counts[:n_exp, 0].astype(jnp.int32)
    tiles_per_expert = (counts + tm - 1) // tm
    tile_end = jnp.cumsum(tiles_per_expert)
    row_start = (tile_end - tiles_per_expert) * tm
    experts = jnp.arange(n_exp, dtype=jnp.int32)[:, None, None]
    pos = jnp.sum(jnp.where(idx[None] == experts, row_start[:, None, None], 0), axis=0) + rank
    pos_flat = pos.reshape(-1)
    pad_first = jnp.concatenate([row_start + counts, tile_end[-1:] * tm])
    pad_count = jnp.concatenate([tiles_per_expert * tm - counts, (n_tiles - tile_end[-1:]) * tm])
    pad_info = jnp.concatenate([pad_first, pad_count]).astype(jnp.int32)
    first_tile = (tile_end - tiles_per_expert).astype(jnp.int32)
    tile_count = tiles_per_expert.astype(jnp.int32)

    xs = _dispatch(xf, pos_flat, pad_info, n_rows=n_tiles * tm, tokens_per_step=512)
    hs = _grouped_mm(xs, w_gu, first_tile, tile_count, tm=tm, tn=896, gated=True, out_dtype=BF16)
    ys = _grouped_mm(hs, w_down, first_tile, tile_count, tm=tm, tn=512, gated=False, out_dtype=F32)
    return _combine_ln(xf, ys, pos_flat, gates[0].reshape(T, 1), gates[1].reshape(T, 1), ln_g, ln_b, tm=256)


def kernel(x, p, fox_w_in, fox_b_f, fox_w_o, gm_w_in, gm_ln_v_g, gm_ln_v_b, gm_w_s, gm_b_s, gm_w_o, ffn_w_gu, ffn_w_down, moe_w_router, moe_w_gu, moe_w_down, ln_mix_g, ln_mix_b, ln_ch_g, ln_ch_b, ple_w_proj, ple_w_gate):
    B, S, D = x.shape
    T = B * S
    H = D // HEAD_DIM
    xf = x.reshape(T, D)
    xb = xf.astype(BF16)
    pf = p.reshape(p.shape[0], T, p.shape[-1])

    w_in = fox_w_in[0]
    qkv = _mm(xb, w_in, n_cols=3 * D, tm=1024, tn=1024, out_dtype=BF16)
    c = _fox_gate(xb, w_in[:, 3 * D:].T.astype(BF16), fox_b_f[0], batch=B, ts=512)
    attn = _fox_attn(qkv, c, batch=B, tq=2048, n_sub=8)
    xf, xb = _mm_res_ln(attn, fox_w_o[0], xf, ln_mix_g[0], ln_mix_b[0], tm=256)
    hid = _swiglu_up(xb, ffn_w_gu[0], tm=1024, tn=512)
    xf, xb = _mm_res_ln(hid, ffn_w_down[0].astype(BF16), xf, ln_ch_g[0], ln_ch_b[0], tm=256)
    xf, xb = _ple(xb, xf, ple_w_gate[0], pf[0], ple_w_proj[0], tm=512, tn=1024)

    z = _mm(xb, gm_w_in[0], n_cols=gm_w_in.shape[2], tm=1024, tn=1024, out_dtype=BF16, act="gelu")
    bias_full = jnp.repeat(gm_b_s[0].T.astype(F32), D // gm_b_s.shape[1], axis=1)
    y = _gm_spatial(z, gm_ln_v_g[0], gm_ln_v_b[0], gm_w_s[0], bias_full, tm=512)
    xf, xb = _mm_res_ln(y, gm_w_o[0], xf, ln_mix_g[1], ln_mix_b[1], tm=256)
    xf, xb = _moe_layer(xf, moe_w_router[0], moe_w_gu[0], moe_w_down[0],
                        ln_ch_g[1], ln_ch_b[1])
    xf, _ = _ple(xb, xf, ple_w_gate[1], pf[1], ple_w_proj[1], tm=512, tn=1024)
    return xf.reshape(B, S, D)
```

```python
import functools

import jax
import jax.numpy as jnp
from jax import lax
from jax.experimental import pallas as pl
from jax.experimental.pallas import tpu as pltpu

F32 = jnp.float32
BF16 = jnp.bfloat16

LN_EPS = 1e-5
DEPTH = 2
ALPHA = (2.0 * DEPTH) ** 0.25
HEAD_DIM = 128
GM_CHUNK = 128
GM_GROUPS = 16
TOP_K = 2
NEG = -1e30
V7X_VMEM_LIMIT_BYTES = 56 * 2**20
EXPERT_ROWS = 16
MOE_TM = 256


def _params(n_axes):
    return pltpu.CompilerParams(dimension_semantics=("arbitrary",) * n_axes,
                                vmem_limit_bytes=V7X_VMEM_LIMIT_BYTES)


def _dot(a, b):
    return jnp.dot(a, b, preferred_element_type=F32)


def _dot_nt(a, b):
    return lax.dot_general(a, b, (((1,), (1,)), ((), ())), preferred_element_type=F32)


def _sigmoid(x):
    return 1.0 / (1.0 + jnp.exp(-x))


def _gelu_tanh(x):
    return 0.5 * x * (1.0 + jnp.tanh(0.7978845608028654 * (x + 0.044715 * (x * x * x))))


def _layer_norm(y, g, b):
    mu = jnp.mean(y, axis=-1, keepdims=True)
    d = y - mu
    var = jnp.mean(d * d, axis=-1, keepdims=True)
    return d * lax.rsqrt(var + LN_EPS) * g + b


def _resident(block_shape, index_map):
    return pl.BlockSpec(block_shape, index_map, pipeline_mode=pl.Buffered(1))


def _bf16_weight(w_ref, cache, fresh):
    if w_ref.dtype == BF16:
        return w_ref[...]
    (cache_ref,) = cache

    @pl.when(fresh)
    def _():
        cache_ref[...] = w_ref[...].astype(BF16)

    return cache_ref[...]


def _weight_cache(w, block_shape):
    return [] if w.dtype == BF16 else [pltpu.VMEM(block_shape, BF16)]


def _mm_kernel(a_ref, w_ref, o_ref, *cache, act):
    y = _dot(a_ref[...], _bf16_weight(w_ref, cache, pl.program_id(1) == 0))
    if act == "gelu":
        y = _gelu_tanh(y)
    o_ref[...] = y.astype(o_ref.dtype)


def _mm(a, w, *, n_cols, tm, tn, out_dtype, act=None):
    M, K = a.shape
    tm, tn = min(tm, M), min(tn, n_cols)
    return pl.pallas_call(
        functools.partial(_mm_kernel, act=act),
        grid=(n_cols // tn, M // tm),
        in_specs=[pl.BlockSpec((tm, K), lambda n, m: (m, 0)),
                  pl.BlockSpec((K, tn), lambda n, m: (0, n))],
        out_specs=pl.BlockSpec((tm, tn), lambda n, m: (m, n)),
        out_shape=jax.ShapeDtypeStruct((M, n_cols), out_dtype),
        scratch_shapes=_weight_cache(w, (K, tn)),
        compiler_params=_params(2),
        name="mm_" + (act or "plain"),
    )(a, w)


def _mm_res_ln_kernel(a_ref, w_ref, r_ref, g_ref, b_ref, of_ref, ob_ref, *cache):
    y = ALPHA * r_ref[...] + _dot(a_ref[...], _bf16_weight(w_ref, cache, pl.program_id(0) == 0))
    y = _layer_norm(y, g_ref[...], b_ref[...])
    of_ref[...] = y
    ob_ref[...] = y.astype(BF16)


def _mm_res_ln(a, w, res, g, b, *, tm):
    M, K = a.shape
    D = w.shape[1]
    tm = min(tm, M)
    return pl.pallas_call(
        _mm_res_ln_kernel,
        grid=(M // tm,),
        in_specs=[pl.BlockSpec((tm, K), lambda m: (m, 0)),
                  _resident((K, D), lambda m: (0, 0)),
                  pl.BlockSpec((tm, D), lambda m: (m, 0)),
                  _resident((1, D), lambda m: (0, 0)),
                  _resident((1, D), lambda m: (0, 0))],
        out_specs=[pl.BlockSpec((tm, D), lambda m: (m, 0)),
                   pl.BlockSpec((tm, D), lambda m: (m, 0))],
        out_shape=[jax.ShapeDtypeStruct((M, D), F32), jax.ShapeDtypeStruct((M, D), BF16)],
        scratch_shapes=_weight_cache(w, (K, D)),
        compiler_params=_params(1),
        name="mm_res_ln",
    )(a, w, res, g.reshape(1, D), b.reshape(1, D))


def _swiglu_up_kernel(x_ref, wg_ref, wu_ref, o_ref, *cache):
    x = x_ref[...]
    fresh = pl.program_id(1) == 0
    g = _dot(x, _bf16_weight(wg_ref, cache[:1], fresh))
    u = _dot(x, _bf16_weight(wu_ref, cache[1:], fresh))
    o_ref[...] = (g * _sigmoid(g) * u).astype(o_ref.dtype)


def _swiglu_up(x, w_gu, *, tm, tn):
    M, K = x.shape
    F = w_gu.shape[1] // 2
    tm, tn = min(tm, M), min(tn, F)
    nf = F // tn
    return pl.pallas_call(
        _swiglu_up_kernel,
        grid=(nf, M // tm),
        in_specs=[pl.BlockSpec((tm, K), lambda n, m: (m, 0)),
                  pl.BlockSpec((K, tn), lambda n, m: (0, n)),
                  pl.BlockSpec((K, tn), lambda n, m: (0, nf + n))],
        out_specs=pl.BlockSpec((tm, tn), lambda n, m: (m, n)),
        out_shape=jax.ShapeDtypeStruct((M, F), BF16),
        scratch_shapes=_weight_cache(w_gu, (K, tn)) * 2,
        compiler_params=_params(2),
        name="swiglu_up",
    )(x, w_gu, w_gu)


def _ple_kernel(xb_ref, wg_ref, p_ref, wp_ref, xf_ref, of_ref, ob_ref, *cache):
    fresh = pl.program_id(1) == 0
    gate = _sigmoid(_dot(xb_ref[...], _bf16_weight(wg_ref, cache[:1], fresh)))
    y = xf_ref[...] + gate * _dot(p_ref[...].astype(BF16), _bf16_weight(wp_ref, cache[1:], fresh))
    of_ref[...] = y
    ob_ref[...] = y.astype(BF16)


def _ple(xb, xf, w_gate, p, w_proj, *, tm, tn):
    M, D = xf.shape
    P = p.shape[1]
    tm, tn = min(tm, M), min(tn, D)
    return pl.pallas_call(
        _ple_kernel,
        grid=(D // tn, M // tm),
        in_specs=[pl.BlockSpec((tm, D), lambda n, m: (m, 0)),
                  pl.BlockSpec((D, tn), lambda n, m: (0, n)),
                  pl.BlockSpec((tm, P), lambda n, m: (m, 0)),
                  pl.BlockSpec((P, tn), lambda n, m: (0, n)),
                  pl.BlockSpec((tm, tn), lambda n, m: (m, n))],
        out_specs=[pl.BlockSpec((tm, tn), lambda n, m: (m, n)),
                   pl.BlockSpec((tm, tn), lambda n, m: (m, n))],
        out_shape=[jax.ShapeDtypeStruct((M, D), F32), jax.ShapeDtypeStruct((M, D), BF16)],
        scratch_shapes=_weight_cache(w_gate, (D, tn)) + _weight_cache(w_proj, (P, tn)),
        compiler_params=_params(2),
        name="ple",
    )(xb, w_gate, p, w_proj, xf)


def _split3_bf16(x):
    hi = x.astype(BF16)
    r = x - hi.astype(F32)
    mid = r.astype(BF16)
    lo = (r - mid.astype(F32)).astype(BF16)
    return hi, mid, lo


def _fox_gate_kernel(x_ref, wft_ref, bf_ref, c_ref, carry_ref, *, ts):
    @pl.when(pl.program_id(1) == 0)
    def _():
        carry_ref[...] = jnp.zeros_like(carry_ref)

    f = _dot_nt(wft_ref[...], x_ref[...]) + bf_ref[...]
    logf = jnp.minimum(f, 0.0) - jnp.log1p(jnp.exp(-jnp.abs(f)))
    row = lax.broadcasted_iota(jnp.int32, (ts, ts), 0)
    col = lax.broadcasted_iota(jnp.int32, (ts, ts), 1)
    tri = jnp.where(row <= col, 1.0, 0.0).astype(BF16)
    hi, mid, lo = _split3_bf16(logf)
    c = _dot(hi, tri) + _dot(mid, tri) + _dot(lo, tri) + carry_ref[...]
    c_ref[0] = c
    carry_ref[...] = c[:, ts - 1:ts]


def _fox_gate(xb, wft, b_f, *, batch, ts):
    T, D = xb.shape
    H = wft.shape[0]
    S = T // batch
    ts = min(ts, S)
    ns = S // ts
    return pl.pallas_call(
        functools.partial(_fox_gate_kernel, ts=ts),
        grid=(batch, ns),
        in_specs=[pl.BlockSpec((ts, D), lambda b, i: (b * ns + i, 0)),
                  _resident((H, D), lambda b, i: (0, 0)),
                  _resident((H, 1), lambda b, i: (0, 0))],
        out_specs=pl.BlockSpec((1, H, ts), lambda b, i: (b, 0, i)),
        out_shape=jax.ShapeDtypeStruct((batch, H, S), F32),
        scratch_shapes=[pltpu.VMEM((H, 1), F32)],
        compiler_params=_params(2),
        name="fox_gate",
    )(xb, wft, b_f.reshape(H, 1).astype(F32))


def _fox_attn_kernel(q_ref, k_ref, v_ref, c_ref, o_ref, kaug, vaug, *, tq, n_sub, scale):
    qi = pl.program_id(2)
    sub = tq // n_sub
    hd = HEAD_DIM
    log2e = 1.4426950408889634
    S = k_ref.shape[0]

    @pl.when(qi == 0)
    def _():
        lane = lax.broadcasted_iota(jnp.int32, (S, hd), 1)
        c_col = jnp.transpose(jnp.broadcast_to(c_ref[0] * (-log2e), (8, S)))[:, 0:1]
        hi, mid, lo = _split3_bf16(c_col)
        extra = jnp.where(lane == 0, hi.astype(F32), jnp.where(lane == 1, mid.astype(F32),
                          jnp.where(lane == 2, lo.astype(F32), 0.0)))
        kaug[:, :hd] = k_ref[...]
        kaug[:, hd:] = extra.astype(BF16)
        vaug[:, :hd] = v_ref[...]
        vaug[:, hd:] = jnp.where(lane == 0, 1.0, 0.0).astype(BF16)

    q_lane = lax.broadcasted_iota(jnp.int32, (sub, hd), 1)
    q_extra = jnp.where(q_lane < 3, 1.0, 0.0).astype(BF16)
    qs = [jnp.concatenate([(q_ref[r * sub:(r + 1) * sub, :].astype(F32) * (scale * log2e)).astype(BF16), q_extra],
                          axis=1) for r in range(n_sub)]

    def update(carry, q, start, width, diagonal):
        m, acc = carry
        s = _dot_nt(q, kaug[pl.ds(start, width), :])
        if diagonal:
            row = lax.broadcasted_iota(jnp.int32, (sub, sub), 0)
            col = lax.broadcasted_iota(jnp.int32, (sub, sub), 1)
            last = jnp.where(row >= col, s[:, width - sub:], NEG)
            s = last if width == sub else jnp.concatenate([s[:, :width - sub], last], axis=1)
        m_new = jnp.maximum(m, jnp.max(s, axis=-1, keepdims=True))
        p = jnp.exp2(s - m_new).astype(BF16)
        acc = jnp.exp2(m - m_new) * acc + _dot(p, vaug[pl.ds(start, width), :])
        return m_new, acc

    def full_block(j, carries):
        start = pl.multiple_of(j * tq, tq)
        return tuple(update(carries[r], qs[r], start, tq, False) for r in range(n_sub))

    init = tuple((jnp.full((sub, 1), NEG, F32), jnp.zeros((sub, 2 * hd), F32)) for _ in range(n_sub))
    carries = lax.fori_loop(0, qi, full_block, init)

    start = pl.multiple_of(qi * tq, tq)
    for r in reversed(range(n_sub)):
        _, acc = update(carries[r], qs[r], start, (r + 1) * sub, True)
        o_ref[r * sub:(r + 1) * sub, :] = (acc[:, :hd] / acc[:, hd:hd + 1]).astype(o_ref.dtype)


def _fox_attn(qkv, c, *, batch, tq, n_sub):
    T, D3 = qkv.shape
    D = D3 // 3
    H = D // HEAD_DIM
    S = T // batch
    tq = min(tq, S)
    nq = S // tq
    c3 = c.reshape(batch * H, 1, S)
    return pl.pallas_call(
        functools.partial(_fox_attn_kernel, tq=tq, n_sub=n_sub, scale=HEAD_DIM ** -0.5),
        grid=(batch, H, nq),
        in_specs=[pl.BlockSpec((tq, HEAD_DIM), lambda b, h, i: (b * nq + i, h)),
                  pl.BlockSpec((S, HEAD_DIM), lambda b, h, i: (b, H + h)),
                  pl.BlockSpec((S, HEAD_DIM), lambda b, h, i: (b, 2 * H + h)),
                  pl.BlockSpec((1, 1, S), lambda b, h, i: (b * H + h, 0, 0))],
        out_specs=pl.BlockSpec((tq, HEAD_DIM), lambda b, h, i: (b * nq + i, h)),
        out_shape=jax.ShapeDtypeStruct((T, D), BF16),
        scratch_shapes=[pltpu.VMEM((S, 2 * HEAD_DIM), BF16), pltpu.VMEM((S, 2 * HEAD_DIM), BF16)],
        compiler_params=_params(3),
        name="fox_attn",
    )(qkv, qkv, qkv, c3)


def _gm_spatial_kernel(u_ref, v_ref, g_ref, b_ref, ws_ref, bs_ref, o_ref, *, n_chunks):
    C = GM_CHUNK
    vn = _layer_norm(v_ref[...].astype(F32), g_ref[...], b_ref[...]).astype(BF16)
    row = lax.broadcasted_iota(jnp.int32, (C, C), 0)
    col = lax.broadcasted_iota(jnp.int32, (C, C), 1)
    for grp in range(ws_ref.shape[0]):
        w = jnp.where(row >= col, ws_ref[grp], 0.0).astype(BF16)
        cols = slice(grp * C, (grp + 1) * C)
        for ch in range(n_chunks):
            rows = slice(ch * C, (ch + 1) * C)
            mixed = _dot(w, vn[rows, cols]) + bs_ref[:, cols]
            o_ref[rows, cols] = (u_ref[rows, cols].astype(F32) * mixed).astype(o_ref.dtype)


def _gm_spatial(z, ln_g, ln_b, w_s, bias_full, *, tm):
    T, W2 = z.shape
    W = W2 // 2
    G, C, _ = w_s.shape
    tm = min(tm, T)
    return pl.pallas_call(
        functools.partial(_gm_spatial_kernel, n_chunks=tm // C),
        grid=(T // tm,),
        in_specs=[pl.BlockSpec((tm, W), lambda i: (i, 0)),
                  pl.BlockSpec((tm, W), lambda i: (i, 1)),
                  _resident((1, W), lambda i: (0, 0)),
                  _resident((1, W), lambda i: (0, 0)),
                  _resident((G, C, C), lambda i: (0, 0, 0)),
                  _resident((C, W), lambda i: (0, 0))],
        out_specs=pl.BlockSpec((tm, W), lambda i: (i, 0)),
        out_shape=jax.ShapeDtypeStruct((T, W), BF16),
        compiler_params=_params(1),
        name="gm_spatial",
    )(z, z, ln_g.reshape(1, W), ln_b.reshape(1, W), w_s, bias_full)


def _router_kernel(x_ref, wrt_ref, o_ref, cnt_ref, carry_ref, *, tm, n_exp):
    @pl.when(pl.program_id(0) == 0)
    def _():
        carry_ref[...] = jnp.zeros_like(carry_ref)

    x = x_ref[...]
    xh = x.astype(BF16)
    xl = (x - xh.astype(F32)).astype(BF16)
    wh, wl = wrt_ref[0], wrt_ref[1]
    logits = _dot_nt(wh, xh) + _dot_nt(wh, xl) + _dot_nt(wl, xh)
    row = lax.broadcasted_iota(jnp.int32, logits.shape, 0)
    logits = jnp.where(row < n_exp, logits, NEG)
    top1 = jnp.max(logits, axis=0, keepdims=True)
    idx1 = jnp.min(jnp.where(logits == top1, row, EXPERT_ROWS), axis=0, keepdims=True)
    rest = jnp.where(row == idx1, NEG, logits)
    top2 = jnp.max(rest, axis=0, keepdims=True)
    idx2 = jnp.min(jnp.where(rest == top2, row, EXPERT_ROWS), axis=0, keepdims=True)
    e = jnp.exp(top2 - top1)
    gate1 = 1.0 / (1.0 + e)
    gate2 = e / (1.0 + e)

    sel = jnp.where((row == idx1) | (row == idx2), 1.0, 0.0)
    r = lax.broadcasted_iota(jnp.int32, (tm, tm), 0)
    c = lax.broadcasted_iota(jnp.int32, (tm, tm), 1)
    tri = jnp.where(r <= c, 1.0, 0.0).astype(BF16)
    incl = _dot(sel.astype(BF16), tri)
    excl = incl - sel + carry_ref[...]
    rank1 = jnp.sum(jnp.where(row == idx1, excl, 0.0), axis=0, keepdims=True)
    rank2 = jnp.sum(jnp.where(row == idx2, excl, 0.0), axis=0, keepdims=True)
    total = carry_ref[...] + incl[:, tm - 1:tm]
    carry_ref[...] = total
    cnt_ref[...] = jnp.broadcast_to(total, cnt_ref.shape)
    o_ref[...] = jnp.concatenate(
        [idx1.astype(F32), idx2.astype(F32), gate1, gate2, rank1, rank2,
         jnp.zeros((2, tm), F32)], axis=0)


def _router(xf, w_router, *, tm):
    T, D = xf.shape
    n_exp = w_router.shape[1]
    tm = min(tm, T)
    wt = jnp.zeros((EXPERT_ROWS, D), F32).at[:n_exp].set(w_router.T.astype(F32))
    wh = wt.astype(BF16)
    wl = (wt - wh.astype(F32)).astype(BF16)
    return pl.pallas_call(
        functools.partial(_router_kernel, tm=tm, n_exp=n_exp),
        grid=(T // tm,),
        in_specs=[pl.BlockSpec((tm, D), lambda i: (i, 0)),
                  _resident((2, EXPERT_ROWS, D), lambda i: (0, 0, 0))],
        out_specs=[pl.BlockSpec((8, tm), lambda i: (0, i)),
                   pl.BlockSpec((EXPERT_ROWS, 128), lambda i: (0, 0))],
        out_shape=[jax.ShapeDtypeStruct((8, T), F32),
                   jax.ShapeDtypeStruct((EXPERT_ROWS, 128), F32)],
        scratch_shapes=[pltpu.VMEM((EXPERT_ROWS, 1), F32)],
        compiler_params=_params(1),
        name="router",
    )(xf, jnp.stack([wh, wl]))


def _grouped_kernel(first_ref, count_ref, w_hbm, x_hbm, o_hbm, wbuf, w_bf, xbuf, obuf, w_sem, in_sem, out_sem,
                    *, tm, tn, n_tiles, gated):
    n_w = 2 if gated else 1
    n, e = pl.program_id(0), pl.program_id(1)
    nb, n_exp = pl.num_programs(0), pl.num_programs(1)
    step = n * n_exp + e
    first, count = first_ref[e], count_ref[e]
    col = pl.multiple_of(n * tn, 128)

    def w_copy(s, j):
        cols = pl.ds(pl.multiple_of((j * nb + s // n_exp) * tn, 128), tn)
        return pltpu.make_async_copy(w_hbm.at[s % n_exp, :, cols], wbuf.at[s % 2, j], w_sem.at[s % 2, j])

    def in_copy(tile, slot):
        rows = pl.ds(pl.multiple_of(tile * tm, tm), tm)
        return pltpu.make_async_copy(x_hbm.at[rows], xbuf.at[slot], in_sem.at[slot])

    def out_copy(tile, slot):
        rows = pl.ds(pl.multiple_of(tile * tm, tm), tm)
        return pltpu.make_async_copy(obuf.at[slot], o_hbm.at[rows, pl.ds(col, tn)], out_sem.at[slot])

    @pl.when(step == 0)
    def _():
        for j in range(n_w):
            w_copy(step, j).start(priority=1)

    @pl.when(step + 1 < nb * n_exp)
    def _():
        for j in range(n_w):
            w_copy(step + 1, j).start(priority=1)

    for j in range(n_w):
        w_copy(step, j).wait()

    @pl.when(count > 0)
    def _():
        @pl.when(step == 0)
        def _():
            in_copy(first, 0).start()

        for j in range(n_w):
            w_bf[j] = wbuf[step % 2, j].astype(BF16)

        def body(i, c):
            slot = i % 2
            in_copy(first + i, slot).wait()

            @pl.when(i + 1 < count)
            def _():
                in_copy(first + i + 1, 1 - slot).start()

            @pl.when(i >= 2)
            def _():
                out_copy(first + i - 2, slot).wait()

            x = xbuf[slot]
            if x.dtype != BF16:
                x = x.astype(BF16)
            y = _dot(x, w_bf[0])
            if gated:
                y = y * _sigmoid(y) * _dot(x, w_bf[1])
            obuf[slot] = y.astype(obuf.dtype)
            out_copy(first + i, slot).start()
            return c

        lax.fori_loop(0, count, body, 0)

        @pl.when(count >= 2)
        def _():
            out_copy(first + count - 2, count % 2).wait()

        out_copy(first + count - 1, (count - 1) % 2).wait()

    next_e = (step + 1) % n_exp

    @pl.when(jnp.logical_and(step + 1 < nb * n_exp, count_ref[next_e] > 0))
    def _():
        in_copy(first_ref[next_e], 0).start()

    @pl.when(e == pl.num_programs(1) - 1)
    def _():
        obuf[0] = jnp.zeros(obuf.shape[1:], obuf.dtype)

        def start_zero(t, c):
            out_copy(t, 0).start()
            return c

        def wait_zero(t, c):
            out_copy(t, 0).wait()
            return c

        lax.fori_loop(first + count, n_tiles, start_zero, 0)
        lax.fori_loop(first + count, n_tiles, wait_zero, 0)


def _grouped_mm(xs, w, first_tile, tile_count, *, tm, tn, gated, out_dtype):
    P, K = xs.shape
    E = w.shape[0]
    N = w.shape[2] // 2 if gated else w.shape[2]
    tn = min(tn, N)
    n_w = 2 if gated else 1
    return pl.pallas_call(
        functools.partial(_grouped_kernel, tm=tm, tn=tn, n_tiles=P // tm, gated=gated),
        grid_spec=pltpu.PrefetchScalarGridSpec(
            num_scalar_prefetch=2,
            grid=(N // tn, E),
            in_specs=[pl.BlockSpec(memory_space=pl.ANY), pl.BlockSpec(memory_space=pl.ANY)],
            out_specs=pl.BlockSpec(memory_space=pl.ANY),
            scratch_shapes=[pltpu.VMEM((2, n_w, K, tn), w.dtype), pltpu.VMEM((n_w, K, tn), BF16),
                            pltpu.VMEM((2, tm, K), xs.dtype), pltpu.VMEM((2, tm, tn), out_dtype),
                            pltpu.SemaphoreType.DMA((2, n_w)),
                            pltpu.SemaphoreType.DMA((2,)), pltpu.SemaphoreType.DMA((2,))]),
        out_shape=jax.ShapeDtypeStruct((P, N), out_dtype),
        compiler_params=_params(2),
        name="moe_up" if gated else "moe_down",
    )(first_tile, tile_count, w, xs)


def _dispatch_kernel(pos_ref, pad_ref, x_ref, xs_hbm, zero_ref, sem, *, tokens_per_step, n_tokens, n_groups):
    base = pl.program_id(0) * tokens_per_step

    def token_copy(j, k):
        dst = pos_ref[k * n_tokens + base + j]
        return pltpu.make_async_copy(x_ref.at[pl.ds(j, 1)], xs_hbm.at[pl.ds(dst, 1)], sem)

    def pad_copy(dst):
        return pltpu.make_async_copy(zero_ref, xs_hbm.at[pl.ds(dst, 1)], sem)

    @pl.when(pl.program_id(0) == 0)
    def _():
        zero_ref[...] = jnp.zeros_like(zero_ref)
        for e in range(n_groups):
            first, count = pad_ref[e], pad_ref[n_groups + e]

            def start_pad(j, c):
                pad_copy(first + j).start()
                return c

            def wait_pad(j, c):
                pad_copy(first + j).wait()
                return c

            lax.fori_loop(0, count, start_pad, 0)
            lax.fori_loop(0, count, wait_pad, 0)

    def start_rows(j, c):
        for k in range(TOP_K):
            token_copy(j, k).start()
        return c

    def wait_rows(j, c):
        for k in range(TOP_K):
            token_copy(j, k).wait()
        return c

    lax.fori_loop(0, tokens_per_step, start_rows, 0, unroll=16)
    lax.fori_loop(0, tokens_per_step, wait_rows, 0, unroll=16)


def _dispatch(xf, pos_flat, pad_info, *, n_rows, tokens_per_step):
    T, D = xf.shape
    tokens_per_step = min(tokens_per_step, T)
    return pl.pallas_call(
        functools.partial(_dispatch_kernel, tokens_per_step=tokens_per_step, n_tokens=T,
                          n_groups=pad_info.shape[0] // 2),
        grid_spec=pltpu.PrefetchScalarGridSpec(
            num_scalar_prefetch=2,
            grid=(T // tokens_per_step,),
            in_specs=[pl.BlockSpec((tokens_per_step, D), lambda i, pos, pad: (i, 0))],
            out_specs=pl.BlockSpec(memory_space=pl.ANY),
            scratch_shapes=[pltpu.VMEM((1, D), F32), pltpu.SemaphoreType.DMA(())]),
        out_shape=jax.ShapeDtypeStruct((n_rows, D), F32),
        compiler_params=_params(1),
        name="moe_dispatch",
    )(pos_flat, pad_info, xf)


def _combine_ln_ple_kernel(pos_ref, r_ref, g1_ref, g2_ref, g_ref, b_ref, p_ref, wg_ref, wp_ref, ys_hbm, o_ref,
                           buf, sem, wg_bf, wp_bf, *, tm, n_tokens):
    i = pl.program_id(0)

    def row_copy(tile, slot, j, k):
        src = pos_ref[k * n_tokens + tile * tm + j]
        return pltpu.make_async_copy(ys_hbm.at[pl.ds(src, 1)], buf.at[slot, k, pl.ds(j, 1)], sem.at[slot])

    def fetch(tile, slot):
        for j in range(tm):
            for k in range(TOP_K):
                row_copy(tile, slot, j, k).start()

    def drain(tile, slot):
        for j in range(tm):
            for k in range(TOP_K):
                row_copy(tile, slot, j, k).wait()

    @pl.when(i == 0)
    def _():
        fetch(0, 0)
        wg_bf[...] = wg_ref[...].astype(BF16)
        wp_bf[...] = wp_ref[...].astype(BF16)

    last = pl.num_programs(0) - 1
    slot = i % 2
    fetch(jnp.minimum(i + 1, last), 1 - slot)
    drain(i, slot)
    ch = g1_ref[...] * buf[slot, 0] + g2_ref[...] * buf[slot, 1]
    y = _layer_norm(ALPHA * r_ref[...] + ch, g_ref[...], b_ref[...])
    gate = _sigmoid(_dot(y.astype(BF16), wg_bf[...]))
    o_ref[...] = y + gate * _dot(p_ref[...].astype(BF16), wp_bf[...])

    @pl.when(i == last)
    def _():
        drain(last, 1 - slot)


def _combine_ln_ple(res, ys, pos_flat, g1, g2, g, b, p, w_gate, w_proj, *, tm):
    M, D = res.shape
    P = p.shape[1]
    tm = min(tm, M)
    tile = pl.BlockSpec((tm, D), lambda m, pos: (m, 0))
    col = pl.BlockSpec((tm, 1), lambda m, pos: (m, 0))
    vec = _resident((1, D), lambda m, pos: (0, 0))
    return pl.pallas_call(
        functools.partial(_combine_ln_ple_kernel, tm=tm, n_tokens=M),
        grid_spec=pltpu.PrefetchScalarGridSpec(
            num_scalar_prefetch=1,
            grid=(M // tm,),
            in_specs=[tile, col, col, vec, vec, pl.BlockSpec((tm, P), lambda m, pos: (m, 0)),
                      _resident((D, D), lambda m, pos: (0, 0)), _resident((P, D), lambda m, pos: (0, 0)),
                      pl.BlockSpec(memory_space=pl.ANY)],
            out_specs=tile,
            scratch_shapes=[pltpu.VMEM((2, TOP_K, tm, D), F32), pltpu.SemaphoreType.DMA((2,)),
                            pltpu.VMEM((D, D), BF16), pltpu.VMEM((P, D), BF16)]),
        out_shape=jax.ShapeDtypeStruct((M, D), F32),
        compiler_params=_params(1),
        name="moe_combine_ln_ple",
    )(pos_flat, res, g1, g2, g.reshape(1, D), b.reshape(1, D), p, w_gate, w_proj, ys)


def _moe_layer(xf, w_router, w_gu, w_down, ln_g, ln_b, p, w_gate, w_proj):
    T, D = xf.shape
    n_exp = w_router.shape[1]
    tm = min(MOE_TM, T)
    n_tiles = (T * TOP_K) // tm + n_exp

    route, counts = _router(xf, w_router, tm=512)
    idx = route[0:2].astype(jnp.int32)
    gates = route[2:4]
    rank = route[4:6].astype(jnp.int32)
    counts = counts[:n_exp, 0].astype(jnp.int32)
    tiles_per_expert = (counts + tm - 1) // tm
    tile_end = jnp.cumsum(tiles_per_expert)
    row_start = (tile_end - tiles_per_expert) * tm
    experts = jnp.arange(n_exp, dtype=jnp.int32)[:, None, None]
    pos = jnp.sum(jnp.where(idx[None] == experts, row_start[:, None, None], 0), axis=0) + rank
    pos_flat = pos.reshape(-1)
    pad_first = jnp.concatenate([row_start + counts, tile_end[-1:] * tm])
    pad_count = jnp.concatenate([tiles_per_expert * tm - counts, (n_tiles - tile_end[-1:]) * tm])
    pad_info = jnp.concatenate([pad_first, pad_count]).astype(jnp.int32)
    first_tile = (tile_end - tiles_per_expert).astype(jnp.int32)
    tile_count = tiles_per_expert.astype(jnp.int32)

    xs = _dispatch(xf, pos_flat, pad_info, n_rows=n_tiles * tm, tokens_per_step=512)
    hs = _grouped_mm(xs, w_gu, first_tile, tile_count, tm=tm, tn=896, gated=True, out_dtype=BF16)
    ys = _grouped_mm(hs, w_down, first_tile, tile_count, tm=tm, tn=512, gated=False, out_dtype=F32)
    return _combine_ln_ple(xf, ys, pos_flat, gates[0].reshape(T, 1), gates[1].reshape(T, 1), ln_g, ln_b,
                           p, w_gate, w_proj, tm=256)


def kernel(x, p, fox_w_in, fox_b_f, fox_w_o, gm_w_in, gm_ln_v_g, gm_ln_v_b, gm_w_s, gm_b_s, gm_w_o, ffn_w_gu, ffn_w_down, moe_w_router, moe_w_gu, moe_w_down, ln_mix_g, ln_mix_b, ln_ch_g, ln_ch_b, ple_w_proj, ple_w_gate):
    B, S, D = x.shape
    T = B * S
    H = D // HEAD_DIM
    xf = x.reshape(T, D)
    xb = xf.astype(BF16)
    pf = p.reshape(p.shape[0], T, p.shape[-1])

    w_in = fox_w_in[0]
    qkv = _mm(xb, w_in, n_cols=3 * D, tm=1024, tn=1024, out_dtype=BF16)
    c = _fox_gate(xb, w_in[:, 3 * D:].T.astype(BF16), fox_b_f[0], batch=B, ts=512)
    attn = _fox_attn(qkv, c, batch=B, tq=2048, n_sub=8)
    xf, xb = _mm_res_ln(attn, fox_w_o[0], xf, ln_mix_g[0], ln_mix_b[0], tm=256)
    hid = _swiglu_up(xb, ffn_w_gu[0], tm=1024, tn=512)
    xf, xb = _mm_res_ln(hid, ffn_w_down[0].astype(BF16), xf, ln_ch_g[0], ln_ch_b[0], tm=256)
    xf, xb = _ple(xb, xf, ple_w_gate[0], pf[0], ple_w_proj[0], tm=512, tn=1024)

    z = _mm(xb, gm_w_in[0], n_cols=gm_w_in.shape[2], tm=1024, tn=1024, out_dtype=BF16, act="gelu")
    bias_full = jnp.repeat(gm_b_s[0].T.astype(F32), D // gm_b_s.shape[1], axis=1)
    y = _gm_spatial(z, gm_ln_v_g[0], gm_ln_v_b[0], gm_w_s[0], bias_full, tm=512)
    xf, xb = _mm_res_ln(y, gm_w_o[0], xf, ln_mix_g[1], ln_mix_b[1], tm=256)
    out = _moe_layer(xf, moe_w_router[0], moe_w_gu[0], moe_w_down[0], ln_ch_g[1], ln_ch_b[1],
                     pf[1], ple_w_gate[1], ple_w_proj[1])
    return out.reshape(B, S, D)
```

```python
import functools

import jax
import jax.numpy as jnp
from jax import lax
from jax.experimental import pallas as pl
from jax.experimental.pallas import tpu as pltpu

F32 = jnp.float32
BF16 = jnp.bfloat16

LN_EPS = 1e-5
DEPTH = 2
ALPHA = (2.0 * DEPTH) ** 0.25
HEAD_DIM = 128
GM_CHUNK = 128
GM_GROUPS = 16
TOP_K = 2
NEG = -1e30
V7X_VMEM_LIMIT_BYTES = 56 * 2**20
EXPERT_ROWS = 16
MOE_TM = 256
ROW_GROUP = 256


def _params(n_axes):
    return pltpu.CompilerParams(dimension_semantics=("arbitrary",) * n_axes,
                                vmem_limit_bytes=V7X_VMEM_LIMIT_BYTES)


def _dot(a, b):
    return jnp.dot(a, b, preferred_element_type=F32)


def _dot_nt(a, b):
    return lax.dot_general(a, b, (((1,), (1,)), ((), ())), preferred_element_type=F32)


def _sigmoid(x):
    return 1.0 / (1.0 + jnp.exp(-x))


def _gelu_tanh(x):
    return 0.5 * x * (1.0 + jnp.tanh(0.7978845608028654 * (x + 0.044715 * (x * x * x))))


def _layer_norm(y, g, b):
    mu = jnp.mean(y, axis=-1, keepdims=True)
    d = y - mu
    var = jnp.mean(d * d, axis=-1, keepdims=True)
    return d * lax.rsqrt(var + LN_EPS) * g + b


def _resident(block_shape, index_map):
    return pl.BlockSpec(block_shape, index_map, pipeline_mode=pl.Buffered(1))


def _bf16_weight(w_ref, cache, fresh):
    if w_ref.dtype == BF16:
        return w_ref[...]
    (cache_ref,) = cache

    @pl.when(fresh)
    def _():
        cache_ref[...] = w_ref[...].astype(BF16)

    return cache_ref[...]


def _weight_cache(w, block_shape):
    return [] if w.dtype == BF16 else [pltpu.VMEM(block_shape, BF16)]


def _row_groups(n_rows):
    group = min(ROW_GROUP, n_rows)
    return [slice(r, r + group) for r in range(0, n_rows, group)]


def _mm_kernel(a_ref, w_ref, o_ref, *cache, act):
    w = _bf16_weight(w_ref, cache, pl.program_id(1) == 0)
    for rows in _row_groups(a_ref.shape[0]):
        a = a_ref[rows, :]
        if a.dtype != BF16:
            a = a.astype(BF16)
        y = _dot(a, w)
        if act == "gelu":
            y = _gelu_tanh(y)
        o_ref[rows, :] = y.astype(o_ref.dtype)


def _mm(a, w, *, n_cols, tm, tn, out_dtype, act=None):
    M, K = a.shape
    tm, tn = min(tm, M), min(tn, n_cols)
    return pl.pallas_call(
        functools.partial(_mm_kernel, act=act),
        grid=(n_cols // tn, M // tm),
        in_specs=[pl.BlockSpec((tm, K), lambda n, m: (m, 0)),
                  pl.BlockSpec((K, tn), lambda n, m: (0, n))],
        out_specs=pl.BlockSpec((tm, tn), lambda n, m: (m, n)),
        out_shape=jax.ShapeDtypeStruct((M, n_cols), out_dtype),
        scratch_shapes=_weight_cache(w, (K, tn)),
        compiler_params=_params(2),
        name="mm_" + (act or "plain"),
    )(a, w)


def _mm_res_ln_kernel(a_ref, w_ref, r_ref, g_ref, b_ref, of_ref, ob_ref, *cache):
    w = _bf16_weight(w_ref, cache, pl.program_id(0) == 0)
    for rows in _row_groups(a_ref.shape[0]):
        y = ALPHA * r_ref[rows, :] + _dot(a_ref[rows, :], w)
        y = _layer_norm(y, g_ref[...], b_ref[...])
        of_ref[rows, :] = y
        ob_ref[rows, :] = y.astype(BF16)


def _mm_res_ln(a, w, res, g, b, *, tm):
    M, K = a.shape
    D = w.shape[1]
    tm = min(tm, M)
    return pl.pallas_call(
        _mm_res_ln_kernel,
        grid=(M // tm,),
        in_specs=[pl.BlockSpec((tm, K), lambda m: (m, 0)),
                  _resident((K, D), lambda m: (0, 0)),
                  pl.BlockSpec((tm, D), lambda m: (m, 0)),
                  _resident((1, D), lambda m: (0, 0)),
                  _resident((1, D), lambda m: (0, 0))],
        out_specs=[pl.BlockSpec((tm, D), lambda m: (m, 0)),
                   pl.BlockSpec((tm, D), lambda m: (m, 0))],
        out_shape=[jax.ShapeDtypeStruct((M, D), F32), jax.ShapeDtypeStruct((M, D), BF16)],
        scratch_shapes=_weight_cache(w, (K, D)),
        compiler_params=_params(1),
        name="mm_res_ln",
    )(a, w, res, g.reshape(1, D), b.reshape(1, D))


def _swiglu_up_kernel(x_ref, wg_ref, wu_ref, o_ref, *cache):
    fresh = pl.program_id(1) == 0
    wg = _bf16_weight(wg_ref, cache[:1], fresh)
    wu = _bf16_weight(wu_ref, cache[1:], fresh)
    for rows in _row_groups(x_ref.shape[0]):
        x = x_ref[rows, :]
        g = _dot(x, wg)
        u = _dot(x, wu)
        o_ref[rows, :] = (g * _sigmoid(g) * u).astype(o_ref.dtype)


def _swiglu_up(x, w_gu, *, tm, tn):
    M, K = x.shape
    F = w_gu.shape[1] // 2
    tm, tn = min(tm, M), min(tn, F)
    nf = F // tn
    return pl.pallas_call(
        _swiglu_up_kernel,
        grid=(nf, M // tm),
        in_specs=[pl.BlockSpec((tm, K), lambda n, m: (m, 0)),
                  pl.BlockSpec((K, tn), lambda n, m: (0, n)),
                  pl.BlockSpec((K, tn), lambda n, m: (0, nf + n))],
        out_specs=pl.BlockSpec((tm, tn), lambda n, m: (m, n)),
        out_shape=jax.ShapeDtypeStruct((M, F), BF16),
        scratch_shapes=_weight_cache(w_gu, (K, tn)) * 2,
        compiler_params=_params(2),
        name="swiglu_up",
    )(x, w_gu, w_gu)


def _ple_kernel(xb_ref, wg_ref, p_ref, wp_ref, xf_ref, of_ref, ob_ref, *cache):
    fresh = pl.program_id(1) == 0
    wg = _bf16_weight(wg_ref, cache[:1], fresh)
    wp = _bf16_weight(wp_ref, cache[1:], fresh)
    for rows in _row_groups(xb_ref.shape[0]):
        gate = _sigmoid(_dot(xb_ref[rows, :], wg))
        y = xf_ref[rows, :] + gate * _dot(p_ref[rows, :].astype(BF16), wp)
        of_ref[rows, :] = y
        ob_ref[rows, :] = y.astype(BF16)


def _ple(xb, xf, w_gate, p, w_proj, *, layer, tm, tn):
    M, D = xf.shape
    P = p.shape[2]
    tm, tn = min(tm, M), min(tn, D)
    return pl.pallas_call(
        _ple_kernel,
        grid=(D // tn, M // tm),
        in_specs=[pl.BlockSpec((tm, D), lambda n, m: (m, 0)),
                  pl.BlockSpec((None, D, tn), lambda n, m: (layer, 0, n)),
                  pl.BlockSpec((None, tm, P), lambda n, m: (layer, m, 0)),
                  pl.BlockSpec((None, P, tn), lambda n, m: (layer, 0, n)),
                  pl.BlockSpec((tm, tn), lambda n, m: (m, n))],
        out_specs=[pl.BlockSpec((tm, tn), lambda n, m: (m, n)),
                   pl.BlockSpec((tm, tn), lambda n, m: (m, n))],
        out_shape=[jax.ShapeDtypeStruct((M, D), F32), jax.ShapeDtypeStruct((M, D), BF16)],
        scratch_shapes=_weight_cache(w_gate, (D, tn)) + _weight_cache(w_proj, (P, tn)),
        compiler_params=_params(2),
        name="ple",
    )(xb, w_gate, p, w_proj, xf)


def _split3_bf16(x):
    hi = x.astype(BF16)
    r = x - hi.astype(F32)
    mid = r.astype(BF16)
    lo = (r - mid.astype(F32)).astype(BF16)
    return hi, mid, lo


def _fox_gate_kernel(x_ref, w_ref, bf_ref, c_ref, carry_ref, *, ts):
    @pl.when(pl.program_id(1) == 0)
    def _():
        carry_ref[...] = jnp.zeros_like(carry_ref)

    H = bf_ref.shape[0]
    f_all = _dot(x_ref[...].astype(BF16), w_ref[...].astype(BF16))
    f = jnp.transpose(f_all)[:H] + bf_ref[...]
    logf = jnp.minimum(f, 0.0) - jnp.log1p(jnp.exp(-jnp.abs(f)))
    row = lax.broadcasted_iota(jnp.int32, (ts, ts), 0)
    col = lax.broadcasted_iota(jnp.int32, (ts, ts), 1)
    tri = jnp.where(row <= col, 1.0, 0.0).astype(BF16)
    hi, mid, lo = _split3_bf16(logf)
    c = _dot(hi, tri) + _dot(mid, tri) + _dot(lo, tri) + carry_ref[...]
    c_ref[0] = c
    carry_ref[...] = c[:, ts - 1:ts]


def _fox_gate(x, w_in, b_f, *, gate_col, batch, ts):
    T, D = x.shape
    H = b_f.shape[0]
    S = T // batch
    ts = min(ts, S)
    ns = S // ts
    return pl.pallas_call(
        functools.partial(_fox_gate_kernel, ts=ts),
        grid=(batch, ns),
        in_specs=[pl.BlockSpec((ts, D), lambda b, i: (b * ns + i, 0)),
                  _resident((D, 128), lambda b, i: (0, gate_col // 128)),
                  _resident((H, 1), lambda b, i: (0, 0))],
        out_specs=pl.BlockSpec((1, H, ts), lambda b, i: (b, 0, i)),
        out_shape=jax.ShapeDtypeStruct((batch, H, S), F32),
        scratch_shapes=[pltpu.VMEM((H, 1), F32)],
        compiler_params=_params(2),
        name="fox_gate",
    )(x, w_in, b_f.reshape(H, 1).astype(F32))


def _fox_attn_kernel(q_ref, k_ref, v_ref, c_ref, o_ref, kaug, vaug, *, tq, n_sub, scale):
    qi = pl.program_id(2)
    sub = tq // n_sub
    hd = HEAD_DIM
    log2e = 1.4426950408889634
    S = k_ref.shape[0]

    @pl.when(qi == 0)
    def _():
        lane = lax.broadcasted_iota(jnp.int32, (S, hd), 1)
        c_col = jnp.transpose(jnp.broadcast_to(c_ref[0] * (-log2e), (8, S)))[:, 0:1]
        hi, mid, lo = _split3_bf16(c_col)
        extra = jnp.where(lane == 0, hi.astype(F32), jnp.where(lane == 1, mid.astype(F32),
                          jnp.where(lane == 2, lo.astype(F32), 0.0)))
        kaug[:, :hd] = k_ref[...]
        kaug[:, hd:] = extra.astype(BF16)
        vaug[:, :hd] = v_ref[...]
        vaug[:, hd:] = jnp.where(lane == 0, 1.0, 0.0).astype(BF16)

    q_lane = lax.broadcasted_iota(jnp.int32, (sub, hd), 1)
    q_extra = jnp.where(q_lane < 3, 1.0, 0.0).astype(BF16)
    qs = [jnp.concatenate([(q_ref[r * sub:(r + 1) * sub, :].astype(F32) * (scale * log2e)).astype(BF16), q_extra],
                          axis=1) for r in range(n_sub)]

    def update(carry, q, start, width, diagonal):
        m, acc = carry
        s = _dot_nt(q, kaug[pl.ds(start, width), :])
        if diagonal:
            row = lax.broadcasted_iota(jnp.int32, (sub, sub), 0)
            col = lax.broadcasted_iota(jnp.int32, (sub, sub), 1)
            last = jnp.where(row >= col, s[:, width - sub:], NEG)
            s = last if width == sub else jnp.concatenate([s[:, :width - sub], last], axis=1)
        m_new = jnp.maximum(m, jnp.max(s, axis=-1, keepdims=True))
        p = jnp.exp2(s - m_new).astype(BF16)
        acc = jnp.exp2(m - m_new) * acc + _dot(p, vaug[pl.ds(start, width), :])
        return m_new, acc

    def full_block(j, carries):
        start = pl.multiple_of(j * tq, tq)
        return tuple(update(carries[r], qs[r], start, tq, False) for r in range(n_sub))

    init = tuple((jnp.full((sub, 1), NEG, F32), jnp.zeros((sub, 2 * hd), F32)) for _ in range(n_sub))
    carries = lax.fori_loop(0, qi, full_block, init)

    start = pl.multiple_of(qi * tq, tq)
    for r in reversed(range(n_sub)):
        _, acc = update(carries[r], qs[r], start, (r + 1) * sub, True)
        o_ref[r * sub:(r + 1) * sub, :] = (acc[:, :hd] / acc[:, hd:hd + 1]).astype(o_ref.dtype)


def _fox_attn(qkv, c, *, batch, tq, n_sub):
    T, D3 = qkv.shape
    D = D3 // 3
    H = D // HEAD_DIM
    S = T // batch
    tq = min(tq, S)
    nq = S // tq
    c3 = c.reshape(batch * H, 1, S)
    return pl.pallas_call(
        functools.partial(_fox_attn_kernel, tq=tq, n_sub=n_sub, scale=HEAD_DIM ** -0.5),
        grid=(batch, H, nq),
        in_specs=[pl.BlockSpec((tq, HEAD_DIM), lambda b, h, i: (b * nq + i, h)),
                  pl.BlockSpec((S, HEAD_DIM), lambda b, h, i: (b, H + h)),
                  pl.BlockSpec((S, HEAD_DIM), lambda b, h, i: (b, 2 * H + h)),
                  pl.BlockSpec((1, 1, S), lambda b, h, i: (b * H + h, 0, 0))],
        out_specs=pl.BlockSpec((tq, HEAD_DIM), lambda b, h, i: (b * nq + i, h)),
        out_shape=jax.ShapeDtypeStruct((T, D), BF16),
        scratch_shapes=[pltpu.VMEM((S, 2 * HEAD_DIM), BF16), pltpu.VMEM((S, 2 * HEAD_DIM), BF16)],
        compiler_params=_params(3),
        name="fox_attn",
    )(qkv, qkv, qkv, c3)


def _gm_spatial_kernel(u_ref, v_ref, g_ref, b_ref, ws_ref, bs_ref, o_ref, *, n_chunks):
    C = GM_CHUNK
    vn = _layer_norm(v_ref[...].astype(F32), g_ref[...], b_ref[...]).astype(BF16)
    row = lax.broadcasted_iota(jnp.int32, (C, C), 0)
    col = lax.broadcasted_iota(jnp.int32, (C, C), 1)
    for grp in range(ws_ref.shape[0]):
        w = jnp.where(row >= col, ws_ref[grp], 0.0).astype(BF16)
        cols = slice(grp * C, (grp + 1) * C)
        for ch in range(n_chunks):
            rows = slice(ch * C, (ch + 1) * C)
            mixed = _dot(w, vn[rows, cols]) + bs_ref[:, cols]
            o_ref[rows, cols] = (u_ref[rows, cols].astype(F32) * mixed).astype(o_ref.dtype)


def _gm_spatial(z, ln_g, ln_b, w_s, bias_full, *, tm):
    T, W2 = z.shape
    W = W2 // 2
    G, C, _ = w_s.shape
    tm = min(tm, T)
    return pl.pallas_call(
        functools.partial(_gm_spatial_kernel, n_chunks=tm // C),
        grid=(T // tm,),
        in_specs=[pl.BlockSpec((tm, W), lambda i: (i, 0)),
                  pl.BlockSpec((tm, W), lambda i: (i, 1)),
                  _resident((1, W), lambda i: (0, 0)),
                  _resident((1, W), lambda i: (0, 0)),
                  _resident((G, C, C), lambda i: (0, 0, 0)),
                  _resident((C, W), lambda i: (0, 0))],
        out_specs=pl.BlockSpec((tm, W), lambda i: (i, 0)),
        out_shape=jax.ShapeDtypeStruct((T, W), BF16),
        compiler_params=_params(1),
        name="gm_spatial",
    )(z, z, ln_g.reshape(1, W), ln_b.reshape(1, W), w_s, bias_full)


def _router_kernel(x_ref, wrt_ref, o_ref, cnt_ref, carry_ref, *, tm, n_exp):
    @pl.when(pl.program_id(0) == 0)
    def _():
        carry_ref[...] = jnp.zeros_like(carry_ref)

    x = x_ref[...]
    xh = x.astype(BF16)
    xl = (x - xh.astype(F32)).astype(BF16)
    wh, wl = wrt_ref[0], wrt_ref[1]
    logits = _dot_nt(wh, xh) + _dot_nt(wh, xl) + _dot_nt(wl, xh)
    row = lax.broadcasted_iota(jnp.int32, logits.shape, 0)
    logits = jnp.where(row < n_exp, logits, NEG)
    top1 = jnp.max(logits, axis=0, keepdims=True)
    idx1 = jnp.min(jnp.where(logits == top1, row, EXPERT_ROWS), axis=0, keepdims=True)
    rest = jnp.where(row == idx1, NEG, logits)
    top2 = jnp.max(rest, axis=0, keepdims=True)
    idx2 = jnp.min(jnp.where(rest == top2, row, EXPERT_ROWS), axis=0, keepdims=True)
    e = jnp.exp(top2 - top1)
    gate1 = 1.0 / (1.0 + e)
    gate2 = e / (1.0 + e)

    sel = jnp.where((row == idx1) | (row == idx2), 1.0, 0.0)
    r = lax.broadcasted_iota(jnp.int32, (tm, tm), 0)
    c = lax.broadcasted_iota(jnp.int32, (tm, tm), 1)
    tri = jnp.where(r <= c, 1.0, 0.0).astype(BF16)
    incl = _dot(sel.astype(BF16), tri)
    excl = incl - sel + carry_ref[...]
    rank1 = jnp.sum(jnp.where(row == idx1, excl, 0.0), axis=0, keepdims=True)
    rank2 = jnp.sum(jnp.where(row == idx2, excl, 0.0), axis=0, keepdims=True)
    total = carry_ref[...] + incl[:, tm - 1:tm]
    carry_ref[...] = total
    cnt_ref[...] = jnp.broadcast_to(total, cnt_ref.shape)
    o_ref[...] = jnp.concatenate(
        [idx1.astype(F32), idx2.astype(F32), gate1, gate2, rank1, rank2,
         jnp.zeros((2, tm), F32)], axis=0)


def _router(xf, w_router, *, tm):
    T, D = xf.shape
    n_exp = w_router.shape[1]
    tm = min(tm, T)
    wt = jnp.zeros((EXPERT_ROWS, D), F32).at[:n_exp].set(w_router.T.astype(F32))
    wh = wt.astype(BF16)
    wl = (wt - wh.astype(F32)).astype(BF16)
    return pl.pallas_call(
        functools.partial(_router_kernel, tm=tm, n_exp=n_exp),
        grid=(T // tm,),
        in_specs=[pl.BlockSpec((tm, D), lambda i: (i, 0)),
                  _resident((2, EXPERT_ROWS, D), lambda i: (0, 0, 0))],
        out_specs=[pl.BlockSpec((8, tm), lambda i: (0, i)),
                   pl.BlockSpec((EXPERT_ROWS, 128), lambda i: (0, 0))],
        out_shape=[jax.ShapeDtypeStruct((8, T), F32),
                   jax.ShapeDtypeStruct((EXPERT_ROWS, 128), F32)],
        scratch_shapes=[pltpu.VMEM((EXPERT_ROWS, 1), F32)],
        compiler_params=_params(1),
        name="router",
    )(xf, jnp.stack([wh, wl]))


def _grouped_kernel(first_ref, count_ref, w_hbm, x_hbm, o_hbm, wbuf, w_bf, xbuf, obuf, w_sem, in_sem, out_sem,
                    *, tm, tn, n_tiles, gated):
    n_w = 2 if gated else 1
    n, e = pl.program_id(0), pl.program_id(1)
    nb, n_exp = pl.num_programs(0), pl.num_programs(1)
    step = n * n_exp + e
    first, count = first_ref[e], count_ref[e]
    col = pl.multiple_of(n * tn, 128)

    def w_copy(s, j):
        cols = pl.ds(pl.multiple_of((j * nb + s // n_exp) * tn, 128), tn)
        return pltpu.make_async_copy(w_hbm.at[s % n_exp, :, cols], wbuf.at[s % 2, j], w_sem.at[s % 2, j])

    def in_copy(tile, slot):
        rows = pl.ds(pl.multiple_of(tile * tm, tm), tm)
        return pltpu.make_async_copy(x_hbm.at[rows], xbuf.at[slot], in_sem.at[slot])

    def out_copy(tile, slot):
        rows = pl.ds(pl.multiple_of(tile * tm, tm), tm)
        return pltpu.make_async_copy(obuf.at[slot], o_hbm.at[rows, pl.ds(col, tn)], out_sem.at[slot])

    @pl.when(step == 0)
    def _():
        for j in range(n_w):
            w_copy(step, j).start(priority=1)

    @pl.when(step + 1 < nb * n_exp)
    def _():
        for j in range(n_w):
            w_copy(step + 1, j).start(priority=1)

    for j in range(n_w):
        w_copy(step, j).wait()

    @pl.when(count > 0)
    def _():
        @pl.when(step == 0)
        def _():
            in_copy(first, 0).start()

        for j in range(n_w):
            w_bf[j] = wbuf[step % 2, j].astype(BF16)

        def body(i, c):
            slot = i % 2
            in_copy(first + i, slot).wait()

            @pl.when(i + 1 < count)
            def _():
                in_copy(first + i + 1, 1 - slot).start()

            @pl.when(i >= 2)
            def _():
                out_copy(first + i - 2, slot).wait()

            x = xbuf[slot]
            if x.dtype != BF16:
                x = x.astype(BF16)
            y = _dot(x, w_bf[0])
            if gated:
                y = y * _sigmoid(y) * _dot(x, w_bf[1])
            obuf[slot] = y.astype(obuf.dtype)
            out_copy(first + i, slot).start()
            return c

        lax.fori_loop(0, count, body, 0)

        @pl.when(count >= 2)
        def _():
            out_copy(first + count - 2, count % 2).wait()

        out_copy(first + count - 1, (count - 1) % 2).wait()

    next_e = (step + 1) % n_exp

    @pl.when(jnp.logical_and(step + 1 < nb * n_exp, count_ref[next_e] > 0))
    def _():
        in_copy(first_ref[next_e], 0).start()

    @pl.when(e == pl.num_programs(1) - 1)
    def _():
        obuf[0] = jnp.zeros(obuf.shape[1:], obuf.dtype)

        def start_zero(t, c):
            out_copy(t, 0).start()
            return c

        def wait_zero(t, c):
            out_copy(t, 0).wait()
            return c

        lax.fori_loop(first + count, n_tiles, start_zero, 0)
        lax.fori_loop(first + count, n_tiles, wait_zero, 0)


def _grouped_mm(xs, w, first_tile, tile_count, *, tm, tn, gated, out_dtype):
    P, K = xs.shape
    E = w.shape[0]
    N = w.shape[2] // 2 if gated else w.shape[2]
    tn = min(tn, N)
    n_w = 2 if gated else 1
    return pl.pallas_call(
        functools.partial(_grouped_kernel, tm=tm, tn=tn, n_tiles=P // tm, gated=gated),
        grid_spec=pltpu.PrefetchScalarGridSpec(
            num_scalar_prefetch=2,
            grid=(N // tn, E),
            in_specs=[pl.BlockSpec(memory_space=pl.ANY), pl.BlockSpec(memory_space=pl.ANY)],
            out_specs=pl.BlockSpec(memory_space=pl.ANY),
            scratch_shapes=[pltpu.VMEM((2, n_w, K, tn), w.dtype), pltpu.VMEM((n_w, K, tn), BF16),
                            pltpu.VMEM((2, tm, K), xs.dtype), pltpu.VMEM((2, tm, tn), out_dtype),
                            pltpu.SemaphoreType.DMA((2, n_w)),
                            pltpu.SemaphoreType.DMA((2,)), pltpu.SemaphoreType.DMA((2,))]),
        out_shape=jax.ShapeDtypeStruct((P, N), out_dtype),
        compiler_params=_params(2),
        name="moe_up" if gated else "moe_down",
    )(first_tile, tile_count, w, xs)


def _dispatch_kernel(pos_ref, pad_ref, x_ref, xs_hbm, zero_ref, sem, *, tokens_per_step, n_tokens, n_groups):
    base = pl.program_id(0) * tokens_per_step

    def token_copy(j, k):
        dst = pos_ref[k * n_tokens + base + j]
        return pltpu.make_async_copy(x_ref.at[pl.ds(j, 1)], xs_hbm.at[pl.ds(dst, 1)], sem)

    def pad_copy(dst):
        return pltpu.make_async_copy(zero_ref, xs_hbm.at[pl.ds(dst, 1)], sem)

    @pl.when(pl.program_id(0) == 0)
    def _():
        zero_ref[...] = jnp.zeros_like(zero_ref)
        for e in range(n_groups):
            first, count = pad_ref[e], pad_ref[n_groups + e]

            def start_pad(j, c):
                pad_copy(first + j).start()
                return c

            def wait_pad(j, c):
                pad_copy(first + j).wait()
                return c

            lax.fori_loop(0, count, start_pad, 0)
            lax.fori_loop(0, count, wait_pad, 0)

    def start_rows(j, c):
        for k in range(TOP_K):
            token_copy(j, k).start()
        return c

    def wait_rows(j, c):
        for k in range(TOP_K):
            token_copy(j, k).wait()
        return c

    lax.fori_loop(0, tokens_per_step, start_rows, 0, unroll=16)
    lax.fori_loop(0, tokens_per_step, wait_rows, 0, unroll=16)


def _dispatch(xf, pos_flat, pad_info, *, n_rows, tokens_per_step):
    T, D = xf.shape
    tokens_per_step = min(tokens_per_step, T)
    return pl.pallas_call(
        functools.partial(_dispatch_kernel, tokens_per_step=tokens_per_step, n_tokens=T,
                          n_groups=pad_info.shape[0] // 2),
        grid_spec=pltpu.PrefetchScalarGridSpec(
            num_scalar_prefetch=2,
            grid=(T // tokens_per_step,),
            in_specs=[pl.BlockSpec((tokens_per_step, D), lambda i, pos, pad: (i, 0))],
            out_specs=pl.BlockSpec(memory_space=pl.ANY),
            scratch_shapes=[pltpu.VMEM((1, D), F32), pltpu.SemaphoreType.DMA(())]),
        out_shape=jax.ShapeDtypeStruct((n_rows, D), F32),
        compiler_params=_params(1),
        name="moe_dispatch",
    )(pos_flat, pad_info, xf)


def _combine_ln_ple_kernel(pos_ref, r_ref, g1_ref, g2_ref, g_ref, b_ref, p_ref, wg_ref, wp_ref, ys_hbm, o_ref,
                           buf, sem, wg_bf, wp_bf, *, tm, n_tokens):
    i = pl.program_id(0)

    def row_copy(tile, slot, j, k):
        src = pos_ref[k * n_tokens + tile * tm + j]
        return pltpu.make_async_copy(ys_hbm.at[pl.ds(src, 1)], buf.at[slot, k, pl.ds(j, 1)], sem.at[slot])

    def fetch(tile, slot):
        for j in range(tm):
            for k in range(TOP_K):
                row_copy(tile, slot, j, k).start()

    def drain(tile, slot):
        for j in range(tm):
            for k in range(TOP_K):
                row_copy(tile, slot, j, k).wait()

    @pl.when(i == 0)
    def _():
        fetch(0, 0)
        wg_bf[...] = wg_ref[...].astype(BF16)
        wp_bf[...] = wp_ref[...].astype(BF16)

    last = pl.num_programs(0) - 1
    slot = i % 2
    fetch(jnp.minimum(i + 1, last), 1 - slot)
    drain(i, slot)
    ch = g1_ref[...] * buf[slot, 0] + g2_ref[...] * buf[slot, 1]
    y = _layer_norm(ALPHA * r_ref[...] + ch, g_ref[...], b_ref[...])
    gate = _sigmoid(_dot(y.astype(BF16), wg_bf[...]))
    o_ref[...] = y + gate * _dot(p_ref[...].astype(BF16), wp_bf[...])

    @pl.when(i == last)
    def _():
        drain(last, 1 - slot)


def _combine_ln_ple(res, ys, pos_flat, g1, g2, g, b, p, w_gate, w_proj, *, layer, tm):
    M, D = res.shape
    P = p.shape[2]
    tm = min(tm, M)
    tile = pl.BlockSpec((tm, D), lambda m, pos: (m, 0))
    col = pl.BlockSpec((tm, 1), lambda m, pos: (m, 0))
    vec = _resident((1, D), lambda m, pos: (0, 0))
    return pl.pallas_call(
        functools.partial(_combine_ln_ple_kernel, tm=tm, n_tokens=M),
        grid_spec=pltpu.PrefetchScalarGridSpec(
            num_scalar_prefetch=1,
            grid=(M // tm,),
            in_specs=[tile, col, col, vec, vec, pl.BlockSpec((None, tm, P), lambda m, pos: (layer, m, 0)),
                      _resident((None, D, D), lambda m, pos: (layer, 0, 0)),
                      _resident((None, P, D), lambda m, pos: (layer, 0, 0)),
                      pl.BlockSpec(memory_space=pl.ANY)],
            out_specs=tile,
            scratch_shapes=[pltpu.VMEM((2, TOP_K, tm, D), F32), pltpu.SemaphoreType.DMA((2,)),
                            pltpu.VMEM((D, D), BF16), pltpu.VMEM((P, D), BF16)]),
        out_shape=jax.ShapeDtypeStruct((M, D), F32),
        compiler_params=_params(1),
        name="moe_combine_ln_ple",
    )(pos_flat, res, g1, g2, g.reshape(1, D), b.reshape(1, D), p, w_gate, w_proj, ys)


def _moe_layer(xf, w_router, w_gu, w_down, ln_g, ln_b, p, w_gate, w_proj, *, ple_layer):
    T, D = xf.shape
    n_exp = w_router.shape[1]
    tm = min(MOE_TM, T)
    n_tiles = (T * TOP_K) // tm + n_exp

    route, counts = _router(xf, w_router, tm=512)
    idx = route[0:2].astype(jnp.int32)
    gates = route[2:4]
    rank = route[4:6].astype(jnp.int32)
    counts = counts[:n_exp, 0].astype(jnp.int32)
    tiles_per_expert = (counts + tm - 1) // tm
    tile_end = jnp.cumsum(tiles_per_expert)
    row_start = (tile_end - tiles_per_expert) * tm
    experts = jnp.arange(n_exp, dtype=jnp.int32)[:, None, None]
    pos = jnp.sum(jnp.where(idx[None] == experts, row_start[:, None, None], 0), axis=0) + rank
    pos_flat = pos.reshape(-1)
    pad_first = jnp.concatenate([row_start + counts, tile_end[-1:] * tm])
    pad_count = jnp.concatenate([tiles_per_expert * tm - counts, (n_tiles - tile_end[-1:]) * tm])
    pad_info = jnp.concatenate([pad_first, pad_count]).astype(jnp.int32)
    first_tile = (tile_end - tiles_per_expert).astype(jnp.int32)
    tile_count = tiles_per_expert.astype(jnp.int32)

    xs = _dispatch(xf, pos_flat, pad_info, n_rows=n_tiles * tm, tokens_per_step=512)
    hs = _grouped_mm(xs, w_gu, first_tile, tile_count, tm=tm, tn=896, gated=True, out_dtype=BF16)
    ys = _grouped_mm(hs, w_down, first_tile, tile_count, tm=tm, tn=512, gated=False, out_dtype=F32)
    return _combine_ln_ple(xf, ys, pos_flat, gates[0].reshape(T, 1), gates[1].reshape(T, 1), ln_g, ln_b,
                           p, w_gate, w_proj, layer=ple_layer, tm=256)


def kernel(x, p, fox_w_in, fox_b_f, fox_w_o, gm_w_in, gm_ln_v_g, gm_ln_v_b, gm_w_s, gm_b_s, gm_w_o, ffn_w_gu, ffn_w_down, moe_w_router, moe_w_gu, moe_w_down, ln_mix_g, ln_mix_b, ln_ch_g, ln_ch_b, ple_w_proj, ple_w_gate):
    B, S, D = x.shape
    T = B * S
    H = D // HEAD_DIM
    xf = x.reshape(T, D)
    pf = p.reshape(p.shape[0], T, p.shape[-1])

    w_in = fox_w_in[0]
    qkv = _mm(xf, w_in, n_cols=3 * D, tm=1024, tn=1024, out_dtype=BF16)
    c = _fox_gate(xf, w_in, fox_b_f[0], gate_col=3 * D, batch=B, ts=512)
    attn = _fox_attn(qkv, c, batch=B, tq=2048, n_sub=8)
    xf, xb = _mm_res_ln(attn, fox_w_o[0], xf, ln_mix_g[0], ln_mix_b[0], tm=256)
    hid = _swiglu_up(xb, ffn_w_gu[0], tm=1024, tn=512)
    xf, xb = _mm_res_ln(hid, ffn_w_down[0].astype(BF16), xf, ln_ch_g[0], ln_ch_b[0], tm=256)
    xf, xb = _ple(xb, xf, ple_w_gate, pf, ple_w_proj, layer=0, tm=512, tn=1024)

    z = _mm(xb, gm_w_in[0], n_cols=gm_w_in.shape[2], tm=1024, tn=1024, out_dtype=BF16, act="gelu")
    bias_full = jnp.repeat(gm_b_s[0].T.astype(F32), D // gm_b_s.shape[1], axis=1)
    y = _gm_spatial(z, gm_ln_v_g[0], gm_ln_v_b[0], gm_w_s[0], bias_full, tm=512)
    xf, xb = _mm_res_ln(y, gm_w_o[0], xf, ln_mix_g[1], ln_mix_b[1], tm=256)
    out = _moe_layer(xf, moe_w_router[0], moe_w_gu[0], moe_w_down[0], ln_ch_g[1], ln_ch_b[1],
                     pf, ple_w_gate, ple_w_proj, ple_layer=1)
    return out.reshape(B, S, D)
```

```python
import functools

import jax
import jax.numpy as jnp
from jax import lax
from jax.experimental import pallas as pl
from jax.experimental.pallas import tpu as pltpu

F32 = jnp.float32
BF16 = jnp.bfloat16

LN_EPS = 1e-5
DEPTH = 2
ALPHA = (2.0 * DEPTH) ** 0.25
HEAD_DIM = 128
GM_CHUNK = 128
GM_GROUPS = 16
TOP_K = 2
NEG = -1e30
V7X_VMEM_LIMIT_BYTES = 56 * 2**20
EXPERT_ROWS = 16
MOE_TM = 256
ROW_GROUP = 256


def _params(n_axes):
    return pltpu.CompilerParams(dimension_semantics=("arbitrary",) * n_axes,
                                vmem_limit_bytes=V7X_VMEM_LIMIT_BYTES)


def _dot(a, b):
    return jnp.dot(a, b, preferred_element_type=F32)


def _dot_nt(a, b):
    return lax.dot_general(a, b, (((1,), (1,)), ((), ())), preferred_element_type=F32)


def _sigmoid(x):
    return 1.0 / (1.0 + jnp.exp(-x))


def _gelu_tanh(x):
    return 0.5 * x * (1.0 + jnp.tanh(0.7978845608028654 * (x + 0.044715 * (x * x * x))))


def _layer_norm(y, g, b):
    mu = jnp.mean(y, axis=-1, keepdims=True)
    d = y - mu
    var = jnp.mean(d * d, axis=-1, keepdims=True)
    return d * lax.rsqrt(var + LN_EPS) * g + b


def _resident(block_shape, index_map):
    return pl.BlockSpec(block_shape, index_map, pipeline_mode=pl.Buffered(1))


def _bf16_weight(w_ref, cache, fresh):
    if w_ref.dtype == BF16:
        return w_ref[...]
    (cache_ref,) = cache

    @pl.when(fresh)
    def _():
        cache_ref[...] = w_ref[...].astype(BF16)

    return cache_ref[...]


def _weight_cache(w, block_shape):
    return [] if w.dtype == BF16 else [pltpu.VMEM(block_shape, BF16)]


def _row_groups(n_rows):
    group = min(ROW_GROUP, n_rows)
    return [slice(r, r + group) for r in range(0, n_rows, group)]


def _mm_kernel(a_ref, w_ref, o_ref, *cache, act):
    w = _bf16_weight(w_ref, cache, pl.program_id(1) == 0)
    for rows in _row_groups(a_ref.shape[0]):
        a = a_ref[rows, :]
        if a.dtype != BF16:
            a = a.astype(BF16)
        y = _dot(a, w)
        if act == "gelu":
            y = _gelu_tanh(y)
        o_ref[rows, :] = y.astype(o_ref.dtype)


def _mm(a, w, *, n_cols, tm, tn, out_dtype, act=None):
    M, K = a.shape
    tm, tn = min(tm, M), min(tn, n_cols)
    return pl.pallas_call(
        functools.partial(_mm_kernel, act=act),
        grid=(n_cols // tn, M // tm),
        in_specs=[pl.BlockSpec((tm, K), lambda n, m: (m, 0)),
                  pl.BlockSpec((K, tn), lambda n, m: (0, n))],
        out_specs=pl.BlockSpec((tm, tn), lambda n, m: (m, n)),
        out_shape=jax.ShapeDtypeStruct((M, n_cols), out_dtype),
        scratch_shapes=_weight_cache(w, (K, tn)),
        compiler_params=_params(2),
        name="mm_" + (act or "plain"),
    )(a, w)


def _mm_res_ln_kernel(a_ref, w_ref, r_ref, g_ref, b_ref, of_ref, ob_ref, *cache):
    w = _bf16_weight(w_ref, cache, pl.program_id(0) == 0)
    for rows in _row_groups(a_ref.shape[0]):
        y = ALPHA * r_ref[rows, :] + _dot(a_ref[rows, :], w)
        y = _layer_norm(y, g_ref[...], b_ref[...])
        of_ref[rows, :] = y
        ob_ref[rows, :] = y.astype(BF16)


def _mm_res_ln(a, w, res, g, b, *, tm):
    M, K = a.shape
    D = w.shape[1]
    tm = min(tm, M)
    return pl.pallas_call(
        _mm_res_ln_kernel,
        grid=(M // tm,),
        in_specs=[pl.BlockSpec((tm, K), lambda m: (m, 0)),
                  _resident((K, D), lambda m: (0, 0)),
                  pl.BlockSpec((tm, D), lambda m: (m, 0)),
                  _resident((1, D), lambda m: (0, 0)),
                  _resident((1, D), lambda m: (0, 0))],
        out_specs=[pl.BlockSpec((tm, D), lambda m: (m, 0)),
                   pl.BlockSpec((tm, D), lambda m: (m, 0))],
        out_shape=[jax.ShapeDtypeStruct((M, D), F32), jax.ShapeDtypeStruct((M, D), BF16)],
        scratch_shapes=_weight_cache(w, (K, D)),
        compiler_params=_params(1),
        name="mm_res_ln",
    )(a, w, res, g.reshape(1, D), b.reshape(1, D))


def _swiglu_up_kernel(x_ref, wg_ref, wu_ref, o_ref, *cache):
    fresh = pl.program_id(1) == 0
    wg = _bf16_weight(wg_ref, cache[:1], fresh)
    wu = _bf16_weight(wu_ref, cache[1:], fresh)
    for rows in _row_groups(x_ref.shape[0]):
        x = x_ref[rows, :]
        g = _dot(x, wg)
        u = _dot(x, wu)
        o_ref[rows, :] = (g * _sigmoid(g) * u).astype(o_ref.dtype)


def _swiglu_up(x, w_gu, *, tm, tn):
    M, K = x.shape
    F = w_gu.shape[1] // 2
    tm, tn = min(tm, M), min(tn, F)
    nf = F // tn
    return pl.pallas_call(
        _swiglu_up_kernel,
        grid=(nf, M // tm),
        in_specs=[pl.BlockSpec((tm, K), lambda n, m: (m, 0)),
                  pl.BlockSpec((K, tn), lambda n, m: (0, n)),
                  pl.BlockSpec((K, tn), lambda n, m: (0, nf + n))],
        out_specs=pl.BlockSpec((tm, tn), lambda n, m: (m, n)),
        out_shape=jax.ShapeDtypeStruct((M, F), BF16),
        scratch_shapes=_weight_cache(w_gu, (K, tn)) * 2,
        compiler_params=_params(2),
        name="swiglu_up",
    )(x, w_gu, w_gu)


def _ple_kernel(xb_ref, wg_ref, p_ref, wp_ref, xf_ref, of_ref, ob_ref, *cache):
    fresh = pl.program_id(1) == 0
    wg = _bf16_weight(wg_ref, cache[:1], fresh)
    wp = _bf16_weight(wp_ref, cache[1:], fresh)
    for rows in _row_groups(xb_ref.shape[0]):
        gate = _sigmoid(_dot(xb_ref[rows, :], wg))
        y = xf_ref[rows, :] + gate * _dot(p_ref[rows, :].astype(BF16), wp)
        of_ref[rows, :] = y
        ob_ref[rows, :] = y.astype(BF16)


def _ple(xb, xf, w_gate, p, w_proj, *, layer, tm, tn):
    M, D = xf.shape
    P = p.shape[2]
    tm, tn = min(tm, M), min(tn, D)
    return pl.pallas_call(
        _ple_kernel,
        grid=(D // tn, M // tm),
        in_specs=[pl.BlockSpec((tm, D), lambda n, m: (m, 0)),
                  pl.BlockSpec((None, D, tn), lambda n, m: (layer, 0, n)),
                  pl.BlockSpec((None, tm, P), lambda n, m: (layer, m, 0)),
                  pl.BlockSpec((None, P, tn), lambda n, m: (layer, 0, n)),
                  pl.BlockSpec((tm, tn), lambda n, m: (m, n))],
        out_specs=[pl.BlockSpec((tm, tn), lambda n, m: (m, n)),
                   pl.BlockSpec((tm, tn), lambda n, m: (m, n))],
        out_shape=[jax.ShapeDtypeStruct((M, D), F32), jax.ShapeDtypeStruct((M, D), BF16)],
        scratch_shapes=_weight_cache(w_gate, (D, tn)) + _weight_cache(w_proj, (P, tn)),
        compiler_params=_params(2),
        name="ple",
    )(xb, w_gate, p, w_proj, xf)


def _split3_bf16(x):
    hi = x.astype(BF16)
    r = x - hi.astype(F32)
    mid = r.astype(BF16)
    lo = (r - mid.astype(F32)).astype(BF16)
    return hi, mid, lo


def _fox_gate_kernel(x_ref, w_ref, bf_ref, c_ref, carry_ref, *, ts):
    @pl.when(pl.program_id(1) == 0)
    def _():
        carry_ref[...] = jnp.zeros_like(carry_ref)

    H = bf_ref.shape[0]
    f_all = _dot(x_ref[...].astype(BF16), w_ref[...].astype(BF16))
    f = jnp.transpose(f_all)[:H] + bf_ref[...]
    logf = jnp.minimum(f, 0.0) - jnp.log1p(jnp.exp(-jnp.abs(f)))
    row = lax.broadcasted_iota(jnp.int32, (ts, ts), 0)
    col = lax.broadcasted_iota(jnp.int32, (ts, ts), 1)
    tri = jnp.where(row <= col, 1.0, 0.0).astype(BF16)
    hi, mid, lo = _split3_bf16(logf)
    c = _dot(hi, tri) + _dot(mid, tri) + _dot(lo, tri) + carry_ref[...]
    c_ref[0] = c
    carry_ref[...] = c[:, ts - 1:ts]


def _fox_gate(x, w_in, b_f, *, gate_col, batch, ts):
    T, D = x.shape
    H = b_f.shape[0]
    S = T // batch
    ts = min(ts, S)
    ns = S // ts
    return pl.pallas_call(
        functools.partial(_fox_gate_kernel, ts=ts),
        grid=(batch, ns),
        in_specs=[pl.BlockSpec((ts, D), lambda b, i: (b * ns + i, 0)),
                  _resident((D, 128), lambda b, i: (0, gate_col // 128)),
                  _resident((H, 1), lambda b, i: (0, 0))],
        out_specs=pl.BlockSpec((1, H, ts), lambda b, i: (b, 0, i)),
        out_shape=jax.ShapeDtypeStruct((batch, H, S), F32),
        scratch_shapes=[pltpu.VMEM((H, 1), F32)],
        compiler_params=_params(2),
        name="fox_gate",
    )(x, w_in, b_f.reshape(H, 1).astype(F32))


def _fox_attn_kernel(q_ref, k_ref, v_ref, c_ref, o_ref, kaug, vaug, *, tq, n_sub, scale):
    qi = pl.program_id(2)
    sub = tq // n_sub
    hd = HEAD_DIM
    log2e = 1.4426950408889634
    S = k_ref.shape[0]

    @pl.when(qi == 0)
    def _():
        lane = lax.broadcasted_iota(jnp.int32, (S, hd), 1)
        c_col = jnp.transpose(jnp.broadcast_to(c_ref[0] * (-log2e), (8, S)))[:, 0:1]
        hi, mid, lo = _split3_bf16(c_col)
        extra = jnp.where(lane == 0, hi.astype(F32), jnp.where(lane == 1, mid.astype(F32),
                          jnp.where(lane == 2, lo.astype(F32), 0.0)))
        kaug[:, :hd] = k_ref[...]
        kaug[:, hd:] = extra.astype(BF16)
        vaug[:, :hd] = v_ref[...]
        vaug[:, hd:] = jnp.where(lane == 0, 1.0, 0.0).astype(BF16)

    q_lane = lax.broadcasted_iota(jnp.int32, (sub, hd), 1)
    q_extra = jnp.where(q_lane < 3, 1.0, 0.0).astype(BF16)
    qs = [jnp.concatenate([(q_ref[r * sub:(r + 1) * sub, :].astype(F32) * (scale * log2e)).astype(BF16), q_extra],
                          axis=1) for r in range(n_sub)]

    def update(carry, q, start, width, diagonal):
        m, acc = carry
        s = _dot_nt(q, kaug[pl.ds(start, width), :])
        if diagonal:
            row = lax.broadcasted_iota(jnp.int32, (sub, sub), 0)
            col = lax.broadcasted_iota(jnp.int32, (sub, sub), 1)
            last = jnp.where(row >= col, s[:, width - sub:], NEG)
            s = last if width == sub else jnp.concatenate([s[:, :width - sub], last], axis=1)
        m_new = jnp.maximum(m, jnp.max(s, axis=-1, keepdims=True))
        p = jnp.exp2(s - m_new).astype(BF16)
        acc = jnp.exp2(m - m_new) * acc + _dot(p, vaug[pl.ds(start, width), :])
        return m_new, acc

    def full_block(j, carries):
        start = pl.multiple_of(j * tq, tq)
        return tuple(update(carries[r], qs[r], start, tq, False) for r in range(n_sub))

    init = tuple((jnp.full((sub, 1), NEG, F32), jnp.zeros((sub, 2 * hd), F32)) for _ in range(n_sub))
    carries = lax.fori_loop(0, qi, full_block, init)

    start = pl.multiple_of(qi * tq, tq)
    for r in reversed(range(n_sub)):
        _, acc = update(carries[r], qs[r], start, (r + 1) * sub, True)
        o_ref[r * sub:(r + 1) * sub, :] = (acc[:, :hd] / acc[:, hd:hd + 1]).astype(o_ref.dtype)


def _fox_attn(qkv, c, *, batch, tq, n_sub):
    T, D3 = qkv.shape
    D = D3 // 3
    H = D // HEAD_DIM
    S = T // batch
    tq = min(tq, S)
    nq = S // tq
    c3 = c.reshape(batch * H, 1, S)
    return pl.pallas_call(
        functools.partial(_fox_attn_kernel, tq=tq, n_sub=n_sub, scale=HEAD_DIM ** -0.5),
        grid=(batch, H, nq),
        in_specs=[pl.BlockSpec((tq, HEAD_DIM), lambda b, h, i: (b * nq + i, h)),
                  pl.BlockSpec((S, HEAD_DIM), lambda b, h, i: (b, H + h)),
                  pl.BlockSpec((S, HEAD_DIM), lambda b, h, i: (b, 2 * H + h)),
                  pl.BlockSpec((1, 1, S), lambda b, h, i: (b * H + h, 0, 0))],
        out_specs=pl.BlockSpec((tq, HEAD_DIM), lambda b, h, i: (b * nq + i, h)),
        out_shape=jax.ShapeDtypeStruct((T, D), BF16),
        scratch_shapes=[pltpu.VMEM((S, 2 * HEAD_DIM), BF16), pltpu.VMEM((S, 2 * HEAD_DIM), BF16)],
        compiler_params=_params(3),
        name="fox_attn",
    )(qkv, qkv, qkv, c3)


def _gm_spatial_kernel(u_ref, v_ref, g_ref, b_ref, ws_ref, bs_ref, o_ref, *, n_chunks):
    C = GM_CHUNK
    vn = _layer_norm(v_ref[...].astype(F32), g_ref[...], b_ref[...]).astype(BF16)
    row = lax.broadcasted_iota(jnp.int32, (C, C), 0)
    col = lax.broadcasted_iota(jnp.int32, (C, C), 1)
    for grp in range(ws_ref.shape[0]):
        w = jnp.where(row >= col, ws_ref[grp], 0.0).astype(BF16)
        cols = slice(grp * C, (grp + 1) * C)
        for ch in range(n_chunks):
            rows = slice(ch * C, (ch + 1) * C)
            mixed = _dot(w, vn[rows, cols]) + bs_ref[:, cols]
            o_ref[rows, cols] = (u_ref[rows, cols].astype(F32) * mixed).astype(o_ref.dtype)


def _gm_spatial(z, ln_g, ln_b, w_s, bias_full, *, tm):
    T, W2 = z.shape
    W = W2 // 2
    G, C, _ = w_s.shape
    tm = min(tm, T)
    return pl.pallas_call(
        functools.partial(_gm_spatial_kernel, n_chunks=tm // C),
        grid=(T // tm,),
        in_specs=[pl.BlockSpec((tm, W), lambda i: (i, 0)),
                  pl.BlockSpec((tm, W), lambda i: (i, 1)),
                  _resident((1, W), lambda i: (0, 0)),
                  _resident((1, W), lambda i: (0, 0)),
                  _resident((G, C, C), lambda i: (0, 0, 0)),
                  _resident((C, W), lambda i: (0, 0))],
        out_specs=pl.BlockSpec((tm, W), lambda i: (i, 0)),
        out_shape=jax.ShapeDtypeStruct((T, W), BF16),
        compiler_params=_params(1),
        name="gm_spatial",
    )(z, z, ln_g.reshape(1, W), ln_b.reshape(1, W), w_s, bias_full)


def _router_kernel(x_ref, wrt_ref, o_ref, cnt_ref, carry_ref, *, tm, n_exp):
    @pl.when(pl.program_id(0) == 0)
    def _():
        carry_ref[...] = jnp.zeros_like(carry_ref)

    x = x_ref[...]
    xh = x.astype(BF16)
    xl = (x - xh.astype(F32)).astype(BF16)
    wh, wl = wrt_ref[0], wrt_ref[1]
    logits = _dot_nt(wh, xh) + _dot_nt(wh, xl) + _dot_nt(wl, xh)
    row = lax.broadcasted_iota(jnp.int32, logits.shape, 0)
    logits = jnp.where(row < n_exp, logits, NEG)
    top1 = jnp.max(logits, axis=0, keepdims=True)
    idx1 = jnp.min(jnp.where(logits == top1, row, EXPERT_ROWS), axis=0, keepdims=True)
    rest = jnp.where(row == idx1, NEG, logits)
    top2 = jnp.max(rest, axis=0, keepdims=True)
    idx2 = jnp.min(jnp.where(rest == top2, row, EXPERT_ROWS), axis=0, keepdims=True)
    e = jnp.exp(top2 - top1)
    gate1 = 1.0 / (1.0 + e)
    gate2 = e / (1.0 + e)

    sel = jnp.where((row == idx1) | (row == idx2), 1.0, 0.0)
    r = lax.broadcasted_iota(jnp.int32, (tm, tm), 0)
    c = lax.broadcasted_iota(jnp.int32, (tm, tm), 1)
    tri = jnp.where(r <= c, 1.0, 0.0).astype(BF16)
    incl = _dot(sel.astype(BF16), tri)
    excl = incl - sel + carry_ref[...]
    rank1 = jnp.sum(jnp.where(row == idx1, excl, 0.0), axis=0, keepdims=True)
    rank2 = jnp.sum(jnp.where(row == idx2, excl, 0.0), axis=0, keepdims=True)
    total = carry_ref[...] + incl[:, tm - 1:tm]
    carry_ref[...] = total
    cnt_ref[...] = jnp.broadcast_to(total, cnt_ref.shape)
    o_ref[...] = jnp.concatenate(
        [idx1.astype(F32), idx2.astype(F32), gate1, gate2, rank1, rank2,
         jnp.zeros((2, tm), F32)], axis=0)


def _router(xf, w_router, *, tm):
    T, D = xf.shape
    n_exp = w_router.shape[1]
    tm = min(tm, T)
    wt = jnp.zeros((EXPERT_ROWS, D), F32).at[:n_exp].set(w_router.T.astype(F32))
    wh = wt.astype(BF16)
    wl = (wt - wh.astype(F32)).astype(BF16)
    return pl.pallas_call(
        functools.partial(_router_kernel, tm=tm, n_exp=n_exp),
        grid=(T // tm,),
        in_specs=[pl.BlockSpec((tm, D), lambda i: (i, 0)),
                  _resident((2, EXPERT_ROWS, D), lambda i: (0, 0, 0))],
        out_specs=[pl.BlockSpec((8, tm), lambda i: (0, i)),
                   pl.BlockSpec((EXPERT_ROWS, 128), lambda i: (0, 0))],
        out_shape=[jax.ShapeDtypeStruct((8, T), F32),
                   jax.ShapeDtypeStruct((EXPERT_ROWS, 128), F32)],
        scratch_shapes=[pltpu.VMEM((EXPERT_ROWS, 1), F32)],
        compiler_params=_params(1),
        name="router",
    )(xf, jnp.stack([wh, wl]))


def _grouped_kernel(first_ref, count_ref, w_hbm, x_hbm, o_hbm, wbuf, w_bf, xbuf, obuf, w_sem, in_sem, out_sem,
                    *, tm, tn, tpi, n_tiles, gated):
    n_w = 2 if gated else 1
    n, e = pl.program_id(0), pl.program_id(1)
    nb, n_exp = pl.num_programs(0), pl.num_programs(1)
    step = n * n_exp + e
    first, count = first_ref[e], count_ref[e]
    col = pl.multiple_of(n * tn, 128)

    def w_copy(s, j):
        cols = pl.ds(pl.multiple_of((j * nb + s // n_exp) * tn, 128), tn)
        return pltpu.make_async_copy(w_hbm.at[s % n_exp, :, cols], wbuf.at[s % 2, j], w_sem.at[s % 2, j])

    big = tpi * tm
    n_big, odd = count // tpi, count % tpi

    def in_copy(tile, n_rows, slot):
        rows = pl.ds(pl.multiple_of(tile * tm, tm), n_rows)
        return pltpu.make_async_copy(x_hbm.at[rows], xbuf.at[slot, pl.ds(0, n_rows)], in_sem.at[slot])

    def out_copy(tile, n_rows, slot):
        rows = pl.ds(pl.multiple_of(tile * tm, tm), n_rows)
        return pltpu.make_async_copy(obuf.at[slot, pl.ds(0, n_rows)], o_hbm.at[rows, pl.ds(col, tn)],
                                     out_sem.at[slot])

    def start_first_item(tile, n_tiles_in_group):
        @pl.when(n_tiles_in_group >= tpi)
        def _():
            in_copy(tile, big, 0).start()

        if tpi > 1:
            @pl.when(jnp.logical_and(n_tiles_in_group > 0, n_tiles_in_group < tpi))
            def _():
                in_copy(tile, tm, 0).start()

    def compute(n_rows, slot):
        x = xbuf[slot, :n_rows]
        if x.dtype != BF16:
            x = x.astype(BF16)
        y = _dot(x, w_bf[0])
        if gated:
            y = y * _sigmoid(y) * _dot(x, w_bf[1])
        obuf[slot, :n_rows] = y.astype(obuf.dtype)

    @pl.when(step == 0)
    def _():
        for j in range(n_w):
            w_copy(step, j).start(priority=1)

    @pl.when(step + 1 < nb * n_exp)
    def _():
        for j in range(n_w):
            w_copy(step + 1, j).start(priority=1)

    for j in range(n_w):
        w_copy(step, j).wait()

    @pl.when(count > 0)
    def _():
        @pl.when(step == 0)
        def _():
            start_first_item(first, count)

        for j in range(n_w):
            w_bf[j] = wbuf[step % 2, j].astype(BF16)

        def body(k, c):
            slot = k % 2
            tile = first + k * tpi
            in_copy(tile, big, slot).wait()

            @pl.when(k + 1 < n_big)
            def _():
                in_copy(tile + tpi, big, 1 - slot).start()

            if tpi > 1:
                @pl.when(jnp.logical_and(k + 1 == n_big, odd > 0))
                def _():
                    in_copy(tile + tpi, tm, 1 - slot).start()

            @pl.when(k >= 2)
            def _():
                out_copy(tile - 2 * tpi, big, slot).wait()

            compute(big, slot)
            out_copy(tile, big, slot).start()
            return c

        lax.fori_loop(0, n_big, body, 0)

        @pl.when(n_big >= 2)
        def _():
            out_copy(first + (n_big - 2) * tpi, big, n_big % 2).wait()

        if tpi > 1:
            @pl.when(odd > 0)
            def _():
                slot = n_big % 2
                tile = first + n_big * tpi
                in_copy(tile, tm, slot).wait()
                compute(tm, slot)
                out_copy(tile, tm, slot).start()

        @pl.when(n_big >= 1)
        def _():
            out_copy(first + (n_big - 1) * tpi, big, (n_big - 1) % 2).wait()

        if tpi > 1:
            @pl.when(odd > 0)
            def _():
                out_copy(first + n_big * tpi, tm, n_big % 2).wait()

    next_e = (step + 1) % n_exp

    @pl.when(step + 1 < nb * n_exp)
    def _():
        start_first_item(first_ref[next_e], count_ref[next_e])

    @pl.when(e == pl.num_programs(1) - 1)
    def _():
        obuf[0] = jnp.zeros(obuf.shape[1:], obuf.dtype)

        def start_zero(t, c):
            out_copy(t, tm, 0).start()
            return c

        def wait_zero(t, c):
            out_copy(t, tm, 0).wait()
            return c

        lax.fori_loop(first + count, n_tiles, start_zero, 0)
        lax.fori_loop(first + count, n_tiles, wait_zero, 0)


def _grouped_mm(xs, w, first_tile, tile_count, *, tm, tn, tpi, gated, out_dtype):
    assert tpi in (1, 2)
    P, K = xs.shape
    E = w.shape[0]
    N = w.shape[2] // 2 if gated else w.shape[2]
    tn = min(tn, N)
    n_w = 2 if gated else 1
    return pl.pallas_call(
        functools.partial(_grouped_kernel, tm=tm, tn=tn, tpi=tpi, n_tiles=P // tm, gated=gated),
        grid_spec=pltpu.PrefetchScalarGridSpec(
            num_scalar_prefetch=2,
            grid=(N // tn, E),
            in_specs=[pl.BlockSpec(memory_space=pl.ANY), pl.BlockSpec(memory_space=pl.ANY)],
            out_specs=pl.BlockSpec(memory_space=pl.ANY),
            scratch_shapes=[pltpu.VMEM((2, n_w, K, tn), w.dtype), pltpu.VMEM((n_w, K, tn), BF16),
                            pltpu.VMEM((2, tpi * tm, K), xs.dtype), pltpu.VMEM((2, tpi * tm, tn), out_dtype),
                            pltpu.SemaphoreType.DMA((2, n_w)),
                            pltpu.SemaphoreType.DMA((2,)), pltpu.SemaphoreType.DMA((2,))]),
        out_shape=jax.ShapeDtypeStruct((P, N), out_dtype),
        compiler_params=_params(2),
        name="moe_up" if gated else "moe_down",
    )(first_tile, tile_count, w, xs)


def _dispatch_kernel(pos_ref, pad_ref, x_ref, xs_hbm, zero_ref, sem, *, tokens_per_step, n_tokens, n_groups):
    base = pl.program_id(0) * tokens_per_step

    def token_copy(j, k):
        dst = pos_ref[k * n_tokens + base + j]
        return pltpu.make_async_copy(x_ref.at[pl.ds(j, 1)], xs_hbm.at[pl.ds(dst, 1)], sem)

    def pad_copy(dst):
        return pltpu.make_async_copy(zero_ref, xs_hbm.at[pl.ds(dst, 1)], sem)

    @pl.when(pl.program_id(0) == 0)
    def _():
        zero_ref[...] = jnp.zeros_like(zero_ref)
        for e in range(n_groups):
            first, count = pad_ref[e], pad_ref[n_groups + e]

            def start_pad(j, c):
                pad_copy(first + j).start()
                return c

            def wait_pad(j, c):
                pad_copy(first + j).wait()
                return c

            lax.fori_loop(0, count, start_pad, 0)
            lax.fori_loop(0, count, wait_pad, 0)

    def start_rows(j, c):
        for k in range(TOP_K):
            token_copy(j, k).start()
        return c

    def wait_rows(j, c):
        for k in range(TOP_K):
            token_copy(j, k).wait()
        return c

    lax.fori_loop(0, tokens_per_step, start_rows, 0, unroll=16)
    lax.fori_loop(0, tokens_per_step, wait_rows, 0, unroll=16)


def _dispatch(xf, pos_flat, pad_info, *, n_rows, tokens_per_step):
    T, D = xf.shape
    tokens_per_step = min(tokens_per_step, T)
    return pl.pallas_call(
        functools.partial(_dispatch_kernel, tokens_per_step=tokens_per_step, n_tokens=T,
                          n_groups=pad_info.shape[0] // 2),
        grid_spec=pltpu.PrefetchScalarGridSpec(
            num_scalar_prefetch=2,
            grid=(T // tokens_per_step,),
            in_specs=[pl.BlockSpec((tokens_per_step, D), lambda i, pos, pad: (i, 0))],
            out_specs=pl.BlockSpec(memory_space=pl.ANY),
            scratch_shapes=[pltpu.VMEM((1, D), F32), pltpu.SemaphoreType.DMA(())]),
        out_shape=jax.ShapeDtypeStruct((n_rows, D), F32),
        compiler_params=_params(1),
        name="moe_dispatch",
    )(pos_flat, pad_info, xf)


def _combine_ln_ple_kernel(pos_ref, r_ref, g1_ref, g2_ref, g_ref, b_ref, p_ref, wg_ref, wp_ref, ys_hbm, o_ref,
                           buf, sem, wg_bf, wp_bf, *, tm, n_tokens):
    i = pl.program_id(0)

    def row_copy(tile, slot, j, k):
        src = pos_ref[k * n_tokens + tile * tm + j]
        return pltpu.make_async_copy(ys_hbm.at[pl.ds(src, 1)], buf.at[slot, k, pl.ds(j, 1)], sem.at[slot])

    def fetch(tile, slot):
        for j in range(tm):
            for k in range(TOP_K):
                row_copy(tile, slot, j, k).start()

    def drain(tile, slot):
        for j in range(tm):
            for k in range(TOP_K):
                row_copy(tile, slot, j, k).wait()

    @pl.when(i == 0)
    def _():
        fetch(0, 0)
        wg_bf[...] = wg_ref[...].astype(BF16)
        wp_bf[...] = wp_ref[...].astype(BF16)

    last = pl.num_programs(0) - 1
    slot = i % 2
    fetch(jnp.minimum(i + 1, last), 1 - slot)
    drain(i, slot)
    ch = g1_ref[...] * buf[slot, 0] + g2_ref[...] * buf[slot, 1]
    y = _layer_norm(ALPHA * r_ref[...] + ch, g_ref[...], b_ref[...])
    gate = _sigmoid(_dot(y.astype(BF16), wg_bf[...]))
    o_ref[...] = y + gate * _dot(p_ref[...].astype(BF16), wp_bf[...])

    @pl.when(i == last)
    def _():
        drain(last, 1 - slot)


def _combine_ln_ple(res, ys, pos_flat, g1, g2, g, b, p, w_gate, w_proj, *, layer, tm):
    M, D = res.shape
    P = p.shape[2]
    tm = min(tm, M)
    tile = pl.BlockSpec((tm, D), lambda m, pos: (m, 0))
    col = pl.BlockSpec((tm, 1), lambda m, pos: (m, 0))
    vec = _resident((1, D), lambda m, pos: (0, 0))
    return pl.pallas_call(
        functools.partial(_combine_ln_ple_kernel, tm=tm, n_tokens=M),
        grid_spec=pltpu.PrefetchScalarGridSpec(
            num_scalar_prefetch=1,
            grid=(M // tm,),
            in_specs=[tile, col, col, vec, vec, pl.BlockSpec((None, tm, P), lambda m, pos: (layer, m, 0)),
                      _resident((None, D, D), lambda m, pos: (layer, 0, 0)),
                      _resident((None, P, D), lambda m, pos: (layer, 0, 0)),
                      pl.BlockSpec(memory_space=pl.ANY)],
            out_specs=tile,
            scratch_shapes=[pltpu.VMEM((2, TOP_K, tm, D), F32), pltpu.SemaphoreType.DMA((2,)),
                            pltpu.VMEM((D, D), BF16), pltpu.VMEM((P, D), BF16)]),
        out_shape=jax.ShapeDtypeStruct((M, D), F32),
        compiler_params=_params(1),
        name="moe_combine_ln_ple",
    )(pos_flat, res, g1, g2, g.reshape(1, D), b.reshape(1, D), p, w_gate, w_proj, ys)


def _moe_layer(xf, w_router, w_gu, w_down, ln_g, ln_b, p, w_gate, w_proj, *, ple_layer):
    T, D = xf.shape
    n_exp = w_router.shape[1]
    tm = min(MOE_TM, T)
    n_tiles = (T * TOP_K) // tm + n_exp

    route, counts = _router(xf, w_router, tm=512)
    idx = route[0:2].astype(jnp.int32)
    gates = route[2:4]
    rank = route[4:6].astype(jnp.int32)
    counts = counts[:n_exp, 0].astype(jnp.int32)
    tiles_per_expert = (counts + tm - 1) // tm
    tile_end = jnp.cumsum(tiles_per_expert)
    row_start = (tile_end - tiles_per_expert) * tm
    experts = jnp.arange(n_exp, dtype=jnp.int32)[:, None, None]
    pos = jnp.sum(jnp.where(idx[None] == experts, row_start[:, None, None], 0), axis=0) + rank
    pos_flat = pos.reshape(-1)
    pad_first = jnp.concatenate([row_start + counts, tile_end[-1:] * tm])
    pad_count = jnp.concatenate([tiles_per_expert * tm - counts, (n_tiles - tile_end[-1:]) * tm])
    pad_info = jnp.concatenate([pad_first, pad_count]).astype(jnp.int32)
    first_tile = (tile_end - tiles_per_expert).astype(jnp.int32)
    tile_count = tiles_per_expert.astype(jnp.int32)

    xs = _dispatch(xf, pos_flat, pad_info, n_rows=n_tiles * tm, tokens_per_step=512)
    hs = _grouped_mm(xs, w_gu, first_tile, tile_count, tm=tm, tn=896, tpi=2, gated=True, out_dtype=BF16)
    ys = _grouped_mm(hs, w_down, first_tile, tile_count, tm=tm, tn=512, tpi=1, gated=False, out_dtype=F32)
    return _combine_ln_ple(xf, ys, pos_flat, gates[0].reshape(T, 1), gates[1].reshape(T, 1), ln_g, ln_b,
                           p, w_gate, w_proj, layer=ple_layer, tm=256)


def kernel(x, p, fox_w_in, fox_b_f, fox_w_o, gm_w_in, gm_ln_v_g, gm_ln_v_b, gm_w_s, gm_b_s, gm_w_o, ffn_w_gu, ffn_w_down, moe_w_router, moe_w_gu, moe_w_down, ln_mix_g, ln_mix_b, ln_ch_g, ln_ch_b, ple_w_proj, ple_w_gate):
    B, S, D = x.shape
    T = B * S
    H = D // HEAD_DIM
    xf = x.reshape(T, D)
    pf = p.reshape(p.shape[0], T, p.shape[-1])

    w_in = fox_w_in[0]
    qkv = _mm(xf, w_in, n_cols=3 * D, tm=1024, tn=1024, out_dtype=BF16)
    c = _fox_gate(xf, w_in, fox_b_f[0], gate_col=3 * D, batch=B, ts=512)
    attn = _fox_attn(qkv, c, batch=B, tq=2048, n_sub=8)
    xf, xb = _mm_res_ln(attn, fox_w_o[0], xf, ln_mix_g[0], ln_mix_b[0], tm=256)
    hid = _swiglu_up(xb, ffn_w_gu[0], tm=1024, tn=512)
    xf, xb = _mm_res_ln(hid, ffn_w_down[0].astype(BF16), xf, ln_ch_g[0], ln_ch_b[0], tm=256)
    xf, xb = _ple(xb, xf, ple_w_gate, pf, ple_w_proj, layer=0, tm=512, tn=1024)

    z = _mm(xb, gm_w_in[0], n_cols=gm_w_in.shape[2], tm=1024, tn=1024, out_dtype=BF16, act="gelu")
    bias_full = jnp.repeat(gm_b_s[0].T.astype(F32), D // gm_b_s.shape[1], axis=1)
    y = _gm_spatial(z, gm_ln_v_g[0], gm_ln_v_b[0], gm_w_s[0], bias_full, tm=512)
    xf, xb = _mm_res_ln(y, gm_w_o[0], xf, ln_mix_g[1], ln_mix_b[1], tm=256)
    out = _moe_layer(xf, moe_w_router[0], moe_w_gu[0], moe_w_down[0], ln_ch_g[1], ln_ch_b[1],
                     pf, ple_w_gate, ple_w_proj, ple_layer=1)
    return out.reshape(B, S, D)
```

```python
import functools

import jax
import jax.numpy as jnp
from jax import lax
from jax.experimental import pallas as pl
from jax.experimental.pallas import tpu as pltpu

F32 = jnp.float32
BF16 = jnp.bfloat16

LN_EPS = 1e-5
DEPTH = 2
ALPHA = (2.0 * DEPTH) ** 0.25
HEAD_DIM = 128
GM_CHUNK = 128
GM_GROUPS = 16
TOP_K = 2
NEG = -1e30
V7X_VMEM_LIMIT_BYTES = 56 * 2**20
EXPERT_ROWS = 16
MOE_TM = 256
ROW_GROUP = 256


def _params(n_axes):
    return pltpu.CompilerParams(dimension_semantics=("arbitrary",) * n_axes,
                                vmem_limit_bytes=V7X_VMEM_LIMIT_BYTES)


def _dot(a, b):
    return jnp.dot(a, b, preferred_element_type=F32)


def _dot_nt(a, b):
    return lax.dot_general(a, b, (((1,), (1,)), ((), ())), preferred_element_type=F32)


def _sigmoid(x):
    return 1.0 / (1.0 + jnp.exp(-x))


def _gelu_tanh(x):
    return 0.5 * x * (1.0 + jnp.tanh(0.7978845608028654 * (x + 0.044715 * (x * x * x))))


def _layer_norm(y, g, b):
    mu = jnp.mean(y, axis=-1, keepdims=True)
    d = y - mu
    var = jnp.mean(d * d, axis=-1, keepdims=True)
    return d * lax.rsqrt(var + LN_EPS) * g + b


def _resident(block_shape, index_map):
    return pl.BlockSpec(block_shape, index_map, pipeline_mode=pl.Buffered(1))


def _bf16_weight(w_ref, cache, fresh):
    if w_ref.dtype == BF16:
        return w_ref[...]
    (cache_ref,) = cache

    @pl.when(fresh)
    def _():
        cache_ref[...] = w_ref[...].astype(BF16)

    return cache_ref[...]


def _weight_cache(w, block_shape):
    return [] if w.dtype == BF16 else [pltpu.VMEM(block_shape, BF16)]


def _row_groups(n_rows):
    group = min(ROW_GROUP, n_rows)
    return [slice(r, r + group) for r in range(0, n_rows, group)]


def _mm_kernel(a_ref, w_ref, o_ref, *cache, act):
    w = _bf16_weight(w_ref, cache, pl.program_id(1) == 0)
    for rows in _row_groups(a_ref.shape[0]):
        a = a_ref[rows, :]
        if a.dtype != BF16:
            a = a.astype(BF16)
        y = _dot(a, w)
        if act == "gelu":
            y = _gelu_tanh(y)
        o_ref[rows, :] = y.astype(o_ref.dtype)


def _mm(a, w, *, n_cols, tm, tn, out_dtype, act=None):
    M, K = a.shape
    tm, tn = min(tm, M), min(tn, n_cols)
    return pl.pallas_call(
        functools.partial(_mm_kernel, act=act),
        grid=(n_cols // tn, M // tm),
        in_specs=[pl.BlockSpec((tm, K), lambda n, m: (m, 0)),
                  pl.BlockSpec((K, tn), lambda n, m: (0, n))],
        out_specs=pl.BlockSpec((tm, tn), lambda n, m: (m, n)),
        out_shape=jax.ShapeDtypeStruct((M, n_cols), out_dtype),
        scratch_shapes=_weight_cache(w, (K, tn)),
        compiler_params=_params(2),
        name="mm_" + (act or "plain"),
    )(a, w)


def _mm_res_ln_kernel(a_ref, w_ref, r_ref, g_ref, b_ref, of_ref, ob_ref, *cache):
    w = _bf16_weight(w_ref, cache, pl.program_id(0) == 0)
    for rows in _row_groups(a_ref.shape[0]):
        y = ALPHA * r_ref[rows, :] + _dot(a_ref[rows, :], w)
        y = _layer_norm(y, g_ref[...], b_ref[...])
        of_ref[rows, :] = y
        ob_ref[rows, :] = y.astype(BF16)


def _mm_res_ln(a, w, res, g, b, *, tm):
    M, K = a.shape
    D = w.shape[1]
    tm = min(tm, M)
    return pl.pallas_call(
        _mm_res_ln_kernel,
        grid=(M // tm,),
        in_specs=[pl.BlockSpec((tm, K), lambda m: (m, 0)),
                  _resident((K, D), lambda m: (0, 0)),
                  pl.BlockSpec((tm, D), lambda m: (m, 0)),
                  _resident((1, D), lambda m: (0, 0)),
                  _resident((1, D), lambda m: (0, 0))],
        out_specs=[pl.BlockSpec((tm, D), lambda m: (m, 0)),
                   pl.BlockSpec((tm, D), lambda m: (m, 0))],
        out_shape=[jax.ShapeDtypeStruct((M, D), F32), jax.ShapeDtypeStruct((M, D), BF16)],
        scratch_shapes=_weight_cache(w, (K, D)),
        compiler_params=_params(1),
        name="mm_res_ln",
    )(a, w, res, g.reshape(1, D), b.reshape(1, D))


def _swiglu_up_kernel(x_ref, wg_ref, wu_ref, o_ref, *cache):
    fresh = pl.program_id(1) == 0
    wg = _bf16_weight(wg_ref, cache[:1], fresh)
    wu = _bf16_weight(wu_ref, cache[1:], fresh)
    for rows in _row_groups(x_ref.shape[0]):
        x = x_ref[rows, :]
        g = _dot(x, wg)
        u = _dot(x, wu)
        o_ref[rows, :] = (g * _sigmoid(g) * u).astype(o_ref.dtype)


def _swiglu_up(x, w_gu, *, tm, tn):
    M, K = x.shape
    F = w_gu.shape[1] // 2
    tm, tn = min(tm, M), min(tn, F)
    nf = F // tn
    return pl.pallas_call(
        _swiglu_up_kernel,
        grid=(nf, M // tm),
        in_specs=[pl.BlockSpec((tm, K), lambda n, m: (m, 0)),
                  pl.BlockSpec((K, tn), lambda n, m: (0, n)),
                  pl.BlockSpec((K, tn), lambda n, m: (0, nf + n))],
        out_specs=pl.BlockSpec((tm, tn), lambda n, m: (m, n)),
        out_shape=jax.ShapeDtypeStruct((M, F), BF16),
        scratch_shapes=_weight_cache(w_gu, (K, tn)) * 2,
        compiler_params=_params(2),
        name="swiglu_up",
    )(x, w_gu, w_gu)


def _ple_kernel(xb_ref, wg_ref, p_ref, wp_ref, xf_ref, of_ref, ob_ref, *cache):
    fresh = pl.program_id(1) == 0
    wg = _bf16_weight(wg_ref, cache[:1], fresh)
    wp = _bf16_weight(wp_ref, cache[1:], fresh)
    for rows in _row_groups(xb_ref.shape[0]):
        gate = _sigmoid(_dot(xb_ref[rows, :], wg))
        y = xf_ref[rows, :] + gate * _dot(p_ref[rows, :].astype(BF16), wp)
        of_ref[rows, :] = y
        ob_ref[rows, :] = y.astype(BF16)


def _ple(xb, xf, w_gate, p, w_proj, *, layer, tm, tn):
    M, D = xf.shape
    P = p.shape[2]
    tm, tn = min(tm, M), min(tn, D)
    return pl.pallas_call(
        _ple_kernel,
        grid=(D // tn, M // tm),
        in_specs=[pl.BlockSpec((tm, D), lambda n, m: (m, 0)),
                  pl.BlockSpec((None, D, tn), lambda n, m: (layer, 0, n)),
                  pl.BlockSpec((None, tm, P), lambda n, m: (layer, m, 0)),
                  pl.BlockSpec((None, P, tn), lambda n, m: (layer, 0, n)),
                  pl.BlockSpec((tm, tn), lambda n, m: (m, n))],
        out_specs=[pl.BlockSpec((tm, tn), lambda n, m: (m, n)),
                   pl.BlockSpec((tm, tn), lambda n, m: (m, n))],
        out_shape=[jax.ShapeDtypeStruct((M, D), F32), jax.ShapeDtypeStruct((M, D), BF16)],
        scratch_shapes=_weight_cache(w_gate, (D, tn)) + _weight_cache(w_proj, (P, tn)),
        compiler_params=_params(2),
        name="ple",
    )(xb, w_gate, p, w_proj, xf)


def _split3_bf16(x):
    hi = x.astype(BF16)
    r = x - hi.astype(F32)
    mid = r.astype(BF16)
    lo = (r - mid.astype(F32)).astype(BF16)
    return hi, mid, lo


def _fox_gate_kernel(x_ref, w_ref, bf_ref, c_ref, carry_ref, *, ts):
    @pl.when(pl.program_id(1) == 0)
    def _():
        carry_ref[...] = jnp.zeros_like(carry_ref)

    H = bf_ref.shape[0]
    f_all = _dot(x_ref[...].astype(BF16), w_ref[...].astype(BF16))
    f = jnp.transpose(f_all)[:H] + bf_ref[...]
    logf = jnp.minimum(f, 0.0) - jnp.log1p(jnp.exp(-jnp.abs(f)))
    row = lax.broadcasted_iota(jnp.int32, (ts, ts), 0)
    col = lax.broadcasted_iota(jnp.int32, (ts, ts), 1)
    tri = jnp.where(row <= col, 1.0, 0.0).astype(BF16)
    hi, mid, lo = _split3_bf16(logf)
    c = _dot(hi, tri) + _dot(mid, tri) + _dot(lo, tri) + carry_ref[...]
    c_ref[0] = c
    carry_ref[...] = c[:, ts - 1:ts]


def _fox_gate(x, w_in, b_f, *, gate_col, batch, ts):
    T, D = x.shape
    H = b_f.shape[0]
    S = T // batch
    ts = min(ts, S)
    ns = S // ts
    return pl.pallas_call(
        functools.partial(_fox_gate_kernel, ts=ts),
        grid=(batch, ns),
        in_specs=[pl.BlockSpec((ts, D), lambda b, i: (b * ns + i, 0)),
                  _resident((D, 128), lambda b, i: (0, gate_col // 128)),
                  _resident((H, 1), lambda b, i: (0, 0))],
        out_specs=pl.BlockSpec((1, H, ts), lambda b, i: (b, 0, i)),
        out_shape=jax.ShapeDtypeStruct((batch, H, S), F32),
        scratch_shapes=[pltpu.VMEM((H, 1), F32)],
        compiler_params=_params(2),
        name="fox_gate",
    )(x, w_in, b_f.reshape(H, 1).astype(F32))


def _fox_attn_kernel(q_ref, k_ref, v_ref, c_ref, o_ref, kaug, vaug, *, tq, n_sub, scale):
    qi = pl.program_id(2)
    sub = tq // n_sub
    hd = HEAD_DIM
    log2e = 1.4426950408889634
    S = k_ref.shape[0]

    @pl.when(qi == 0)
    def _():
        lane = lax.broadcasted_iota(jnp.int32, (S, hd), 1)
        c_col = jnp.transpose(jnp.broadcast_to(c_ref[0] * (-log2e), (8, S)))[:, 0:1]
        hi, mid, lo = _split3_bf16(c_col)
        extra = jnp.where(lane == 0, hi.astype(F32), jnp.where(lane == 1, mid.astype(F32),
                          jnp.where(lane == 2, lo.astype(F32), 0.0)))
        kaug[:, :hd] = k_ref[...]
        kaug[:, hd:] = extra.astype(BF16)
        vaug[:, :hd] = v_ref[...]
        vaug[:, hd:] = jnp.where(lane == 0, 1.0, 0.0).astype(BF16)

    q_lane = lax.broadcasted_iota(jnp.int32, (sub, hd), 1)
    q_extra = jnp.where(q_lane < 3, 1.0, 0.0).astype(BF16)
    qs = [jnp.concatenate([(q_ref[r * sub:(r + 1) * sub, :].astype(F32) * (scale * log2e)).astype(BF16), q_extra],
                          axis=1) for r in range(n_sub)]

    def update(carry, q, start, width, diagonal):
        m, acc = carry
        s = _dot_nt(q, kaug[pl.ds(start, width), :])
        if diagonal:
            row = lax.broadcasted_iota(jnp.int32, (sub, sub), 0)
            col = lax.broadcasted_iota(jnp.int32, (sub, sub), 1)
            last = jnp.where(row >= col, s[:, width - sub:], NEG)
            s = last if width == sub else jnp.concatenate([s[:, :width - sub], last], axis=1)
        m_new = jnp.maximum(m, jnp.max(s, axis=-1, keepdims=True))
        p = jnp.exp2(s - m_new).astype(BF16)
        acc = jnp.exp2(m - m_new) * acc + _dot(p, vaug[pl.ds(start, width), :])
        return m_new, acc

    def full_block(j, carries):
        start = pl.multiple_of(j * tq, tq)
        return tuple(update(carries[r], qs[r], start, tq, False) for r in range(n_sub))

    init = tuple((jnp.full((sub, 1), NEG, F32), jnp.zeros((sub, 2 * hd), F32)) for _ in range(n_sub))
    carries = lax.fori_loop(0, qi, full_block, init)

    start = pl.multiple_of(qi * tq, tq)
    for r in reversed(range(n_sub)):
        _, acc = update(carries[r], qs[r], start, (r + 1) * sub, True)
        o_ref[r * sub:(r + 1) * sub, :] = (acc[:, :hd] / acc[:, hd:hd + 1]).astype(o_ref.dtype)


def _fox_attn(qkv, c, *, batch, tq, n_sub):
    T, D3 = qkv.shape
    D = D3 // 3
    H = D // HEAD_DIM
    S = T // batch
    tq = min(tq, S)
    nq = S // tq
    c3 = c.reshape(batch * H, 1, S)
    return pl.pallas_call(
        functools.partial(_fox_attn_kernel, tq=tq, n_sub=n_sub, scale=HEAD_DIM ** -0.5),
        grid=(batch, H, nq),
        in_specs=[pl.BlockSpec((tq, HEAD_DIM), lambda b, h, i: (b * nq + i, h)),
                  pl.BlockSpec((S, HEAD_DIM), lambda b, h, i: (b, H + h)),
                  pl.BlockSpec((S, HEAD_DIM), lambda b, h, i: (b, 2 * H + h)),
                  pl.BlockSpec((1, 1, S), lambda b, h, i: (b * H + h, 0, 0))],
        out_specs=pl.BlockSpec((tq, HEAD_DIM), lambda b, h, i: (b * nq + i, h)),
        out_shape=jax.ShapeDtypeStruct((T, D), BF16),
        scratch_shapes=[pltpu.VMEM((S, 2 * HEAD_DIM), BF16), pltpu.VMEM((S, 2 * HEAD_DIM), BF16)],
        compiler_params=_params(3),
        name="fox_attn",
    )(qkv, qkv, qkv, c3)


def _gm_spatial_kernel(u_ref, v_ref, g_ref, b_ref, ws_ref, bs_ref, o_ref, *, n_chunks):
    C = GM_CHUNK
    vn = _layer_norm(v_ref[...].astype(F32), g_ref[...], b_ref[...]).astype(BF16)
    row = lax.broadcasted_iota(jnp.int32, (C, C), 0)
    col = lax.broadcasted_iota(jnp.int32, (C, C), 1)
    for grp in range(ws_ref.shape[0]):
        w = jnp.where(row >= col, ws_ref[grp], 0.0).astype(BF16)
        cols = slice(grp * C, (grp + 1) * C)
        for ch in range(n_chunks):
            rows = slice(ch * C, (ch + 1) * C)
            mixed = _dot(w, vn[rows, cols]) + bs_ref[:, cols]
            o_ref[rows, cols] = (u_ref[rows, cols].astype(F32) * mixed).astype(o_ref.dtype)


def _gm_spatial(z, ln_g, ln_b, w_s, bias_full, *, tm):
    T, W2 = z.shape
    W = W2 // 2
    G, C, _ = w_s.shape
    tm = min(tm, T)
    return pl.pallas_call(
        functools.partial(_gm_spatial_kernel, n_chunks=tm // C),
        grid=(T // tm,),
        in_specs=[pl.BlockSpec((tm, W), lambda i: (i, 0)),
                  pl.BlockSpec((tm, W), lambda i: (i, 1)),
                  _resident((1, W), lambda i: (0, 0)),
                  _resident((1, W), lambda i: (0, 0)),
                  _resident((G, C, C), lambda i: (0, 0, 0)),
                  _resident((C, W), lambda i: (0, 0))],
        out_specs=pl.BlockSpec((tm, W), lambda i: (i, 0)),
        out_shape=jax.ShapeDtypeStruct((T, W), BF16),
        compiler_params=_params(1),
        name="gm_spatial",
    )(z, z, ln_g.reshape(1, W), ln_b.reshape(1, W), w_s, bias_full)


def _router_kernel(x_ref, wrt_ref, o_ref, cnt_ref, carry_ref, *, tm, n_exp):
    @pl.when(pl.program_id(0) == 0)
    def _():
        carry_ref[...] = jnp.zeros_like(carry_ref)

    x = x_ref[...]
    xh = x.astype(BF16)
    xl = (x - xh.astype(F32)).astype(BF16)
    wh, wl = wrt_ref[0], wrt_ref[1]
    logits = _dot_nt(wh, xh) + _dot_nt(wh, xl) + _dot_nt(wl, xh)
    row = lax.broadcasted_iota(jnp.int32, logits.shape, 0)
    logits = jnp.where(row < n_exp, logits, NEG)
    top1 = jnp.max(logits, axis=0, keepdims=True)
    idx1 = jnp.min(jnp.where(logits == top1, row, EXPERT_ROWS), axis=0, keepdims=True)
    rest = jnp.where(row == idx1, NEG, logits)
    top2 = jnp.max(rest, axis=0, keepdims=True)
    idx2 = jnp.min(jnp.where(rest == top2, row, EXPERT_ROWS), axis=0, keepdims=True)
    e = jnp.exp(top2 - top1)
    gate1 = 1.0 / (1.0 + e)
    gate2 = e / (1.0 + e)

    sel = jnp.where((row == idx1) | (row == idx2), 1.0, 0.0)
    r = lax.broadcasted_iota(jnp.int32, (tm, tm), 0)
    c = lax.broadcasted_iota(jnp.int32, (tm, tm), 1)
    tri = jnp.where(r <= c, 1.0, 0.0).astype(BF16)
    incl = _dot(sel.astype(BF16), tri)
    excl = incl - sel + carry_ref[...]
    rank1 = jnp.sum(jnp.where(row == idx1, excl, 0.0), axis=0, keepdims=True)
    rank2 = jnp.sum(jnp.where(row == idx2, excl, 0.0), axis=0, keepdims=True)
    total = carry_ref[...] + incl[:, tm - 1:tm]
    carry_ref[...] = total
    cnt_ref[...] = jnp.broadcast_to(total, cnt_ref.shape)
    o_ref[...] = jnp.concatenate(
        [idx1.astype(F32), idx2.astype(F32), gate1, gate2, rank1, rank2,
         jnp.zeros((2, tm), F32)], axis=0)


def _router(xf, w_router, *, tm):
    T, D = xf.shape
    n_exp = w_router.shape[1]
    tm = min(tm, T)
    wt = jnp.zeros((EXPERT_ROWS, D), F32).at[:n_exp].set(w_router.T.astype(F32))
    wh = wt.astype(BF16)
    wl = (wt - wh.astype(F32)).astype(BF16)
    return pl.pallas_call(
        functools.partial(_router_kernel, tm=tm, n_exp=n_exp),
        grid=(T // tm,),
        in_specs=[pl.BlockSpec((tm, D), lambda i: (i, 0)),
                  _resident((2, EXPERT_ROWS, D), lambda i: (0, 0, 0))],
        out_specs=[pl.BlockSpec((8, tm), lambda i: (0, i)),
                   pl.BlockSpec((EXPERT_ROWS, 128), lambda i: (0, 0))],
        out_shape=[jax.ShapeDtypeStruct((8, T), F32),
                   jax.ShapeDtypeStruct((EXPERT_ROWS, 128), F32)],
        scratch_shapes=[pltpu.VMEM((EXPERT_ROWS, 1), F32)],
        compiler_params=_params(1),
        name="router",
    )(xf, jnp.stack([wh, wl]))


def _grouped_kernel(first_ref, count_ref, w_hbm, x_hbm, o_hbm, wbuf, w_bf, xbuf, obuf, state, w_sem, in_sem, out_sem,
                    *, tm, tn, tpi, n_tiles, gated):
    n_w = 2 if gated else 1
    n, e = pl.program_id(0), pl.program_id(1)
    nb, n_exp = pl.num_programs(0), pl.num_programs(1)
    step = n * n_exp + e
    first, count = first_ref[e], count_ref[e]
    col = pl.multiple_of(n * tn, 128)
    big = tpi * tm
    n_big, odd = count // tpi, count % tpi
    n_items = n_big + odd

    def w_copy(s, j):
        cols = pl.ds(pl.multiple_of((j * nb + s // n_exp) * tn, 128), tn)
        return pltpu.make_async_copy(w_hbm.at[s % n_exp, :, cols], wbuf.at[s % 2, j], w_sem.at[s % 2, j])

    def in_copy(tile, n_rows, slot):
        rows = pl.ds(pl.multiple_of(tile * tm, tm), n_rows)
        return pltpu.make_async_copy(x_hbm.at[rows], xbuf.at[slot, pl.ds(0, n_rows)], in_sem.at[slot])

    def out_copy(tile, n_rows, slot):
        rows = pl.ds(pl.multiple_of(tile * tm, tm), n_rows)
        return pltpu.make_async_copy(obuf.at[slot, pl.ds(0, n_rows)], o_hbm.at[rows, pl.ds(col, tn)],
                                     out_sem.at[slot])

    def for_item_size(tiles, fn):
        @pl.when(tiles == tpi)
        def _():
            fn(big)

        if tpi > 1:
            @pl.when(tiles == 1)
            def _():
                fn(tm)

    def start_first_item_of(group, slot):
        tiles = jnp.minimum(count_ref[group], tpi)
        for_item_size(tiles, lambda n_rows: in_copy(first_ref[group], n_rows, slot).start())

    def start_next_steps_first_item(slot):
        @pl.when(step + 1 < nb * n_exp)
        def _():
            start_first_item_of((step + 1) % n_exp, slot)

    def retire_output(slot):
        for_item_size(state[1 + slot], lambda n_rows: out_copy(state[3 + slot], n_rows, slot).wait())
        state[1 + slot] = 0

    def run_item(tile, n_rows, slot):
        retire_output(slot)
        x = xbuf[slot, :n_rows]
        if x.dtype != BF16:
            x = x.astype(BF16)
        y = _dot(x, w_bf[0])
        if gated:
            y = y * _sigmoid(y) * _dot(x, w_bf[1])
        obuf[slot, :n_rows] = y.astype(obuf.dtype)
        out_copy(tile, n_rows, slot).start()
        state[1 + slot] = n_rows // tm
        state[3 + slot] = tile

    @pl.when(step == 0)
    def _():
        for i in range(5):
            state[i] = 0
        for j in range(n_w):
            w_copy(step, j).start(priority=1)
        start_first_item_of(e, 0)

    @pl.when(step + 1 < nb * n_exp)
    def _():
        for j in range(n_w):
            w_copy(step + 1, j).start(priority=1)

    for j in range(n_w):
        w_copy(step, j).wait()

    issued = state[0]

    @pl.when(count == 0)
    def _():
        start_next_steps_first_item(issued % 2)

    @pl.when(count > 0)
    def _():
        for j in range(n_w):
            w_bf[j] = wbuf[step % 2, j].astype(BF16)

        def body(k, c):
            slot = (issued + k) % 2
            tile = first + k * tpi
            in_copy(tile, big, slot).wait()

            @pl.when(k + 1 < n_big)
            def _():
                in_copy(tile + tpi, big, 1 - slot).start()

            if tpi > 1:
                @pl.when(jnp.logical_and(k + 1 == n_big, odd > 0))
                def _():
                    in_copy(tile + tpi, tm, 1 - slot).start()

            @pl.when(k + 1 == n_items)
            def _():
                start_next_steps_first_item(1 - slot)

            run_item(tile, big, slot)
            return c

        lax.fori_loop(0, n_big, body, 0)

        if tpi > 1:
            @pl.when(odd > 0)
            def _():
                slot = (issued + n_big) % 2
                tile = first + n_big * tpi
                in_copy(tile, tm, slot).wait()
                start_next_steps_first_item(1 - slot)
                run_item(tile, tm, slot)

        state[0] = issued + n_items

    @pl.when(e == n_exp - 1)
    def _():
        retire_output(0)
        retire_output(1)
        obuf[0] = jnp.zeros(obuf.shape[1:], obuf.dtype)

        def start_zero(t, c):
            out_copy(t, tm, 0).start()
            return c

        def wait_zero(t, c):
            out_copy(t, tm, 0).wait()
            return c

        lax.fori_loop(first + count, n_tiles, start_zero, 0)
        lax.fori_loop(first + count, n_tiles, wait_zero, 0)


def _grouped_mm(xs, w, first_tile, tile_count, *, tm, tn, tpi, gated, out_dtype):
    assert tpi in (1, 2)
    P, K = xs.shape
    E = w.shape[0]
    N = w.shape[2] // 2 if gated else w.shape[2]
    tn = min(tn, N)
    n_w = 2 if gated else 1
    return pl.pallas_call(
        functools.partial(_grouped_kernel, tm=tm, tn=tn, tpi=tpi, n_tiles=P // tm, gated=gated),
        grid_spec=pltpu.PrefetchScalarGridSpec(
            num_scalar_prefetch=2,
            grid=(N // tn, E),
            in_specs=[pl.BlockSpec(memory_space=pl.ANY), pl.BlockSpec(memory_space=pl.ANY)],
            out_specs=pl.BlockSpec(memory_space=pl.ANY),
            scratch_shapes=[pltpu.VMEM((2, n_w, K, tn), w.dtype), pltpu.VMEM((n_w, K, tn), BF16),
                            pltpu.VMEM((2, tpi * tm, K), xs.dtype), pltpu.VMEM((2, tpi * tm, tn), out_dtype),
                            pltpu.SMEM((5,), jnp.int32), pltpu.SemaphoreType.DMA((2, n_w)),
                            pltpu.SemaphoreType.DMA((2,)), pltpu.SemaphoreType.DMA((2,))]),
        out_shape=jax.ShapeDtypeStruct((P, N), out_dtype),
        compiler_params=_params(2),
        name="moe_up" if gated else "moe_down",
    )(first_tile, tile_count, w, xs)


def _dispatch_kernel(pos_ref, pad_ref, x_ref, xs_hbm, zero_ref, sem, *, tokens_per_step, n_tokens, n_groups):
    base = pl.program_id(0) * tokens_per_step

    def token_copy(j, k):
        dst = pos_ref[k * n_tokens + base + j]
        return pltpu.make_async_copy(x_ref.at[pl.ds(j, 1)], xs_hbm.at[pl.ds(dst, 1)], sem)

    def pad_copy(dst):
        return pltpu.make_async_copy(zero_ref, xs_hbm.at[pl.ds(dst, 1)], sem)

    @pl.when(pl.program_id(0) == 0)
    def _():
        zero_ref[...] = jnp.zeros_like(zero_ref)
        for e in range(n_groups):
            first, count = pad_ref[e], pad_ref[n_groups + e]

            def start_pad(j, c):
                pad_copy(first + j).start()
                return c

            def wait_pad(j, c):
                pad_copy(first + j).wait()
                return c

            lax.fori_loop(0, count, start_pad, 0)
            lax.fori_loop(0, count, wait_pad, 0)

    def start_rows(j, c):
        for k in range(TOP_K):
            token_copy(j, k).start()
        return c

    def wait_rows(j, c):
        for k in range(TOP_K):
            token_copy(j, k).wait()
        return c

    lax.fori_loop(0, tokens_per_step, start_rows, 0, unroll=16)
    lax.fori_loop(0, tokens_per_step, wait_rows, 0, unroll=16)


def _dispatch(xf, pos_flat, pad_info, *, n_rows, tokens_per_step):
    T, D = xf.shape
    tokens_per_step = min(tokens_per_step, T)
    return pl.pallas_call(
        functools.partial(_dispatch_kernel, tokens_per_step=tokens_per_step, n_tokens=T,
                          n_groups=pad_info.shape[0] // 2),
        grid_spec=pltpu.PrefetchScalarGridSpec(
            num_scalar_prefetch=2,
            grid=(T // tokens_per_step,),
            in_specs=[pl.BlockSpec((tokens_per_step, D), lambda i, pos, pad: (i, 0))],
            out_specs=pl.BlockSpec(memory_space=pl.ANY),
            scratch_shapes=[pltpu.VMEM((1, D), F32), pltpu.SemaphoreType.DMA(())]),
        out_shape=jax.ShapeDtypeStruct((n_rows, D), F32),
        compiler_params=_params(1),
        name="moe_dispatch",
    )(pos_flat, pad_info, xf)


def _combine_ln_ple_kernel(pos_ref, r_ref, g1_ref, g2_ref, g_ref, b_ref, p_ref, wg_ref, wp_ref, ys_hbm, o_ref,
                           buf, sem, wg_bf, wp_bf, *, tm, n_tokens):
    i = pl.program_id(0)

    def row_copy(tile, slot, j, k):
        src = pos_ref[k * n_tokens + tile * tm + j]
        return pltpu.make_async_copy(ys_hbm.at[pl.ds(src, 1)], buf.at[slot, k, pl.ds(j, 1)], sem.at[slot])

    def fetch(tile, slot):
        for j in range(tm):
            for k in range(TOP_K):
                row_copy(tile, slot, j, k).start()

    def drain(tile, slot):
        for j in range(tm):
            for k in range(TOP_K):
                row_copy(tile, slot, j, k).wait()

    @pl.when(i == 0)
    def _():
        fetch(0, 0)
        wg_bf[...] = wg_ref[...].astype(BF16)
        wp_bf[...] = wp_ref[...].astype(BF16)

    last = pl.num_programs(0) - 1
    slot = i % 2
    fetch(jnp.minimum(i + 1, last), 1 - slot)
    drain(i, slot)
    ch = g1_ref[...] * buf[slot, 0] + g2_ref[...] * buf[slot, 1]
    y = _layer_norm(ALPHA * r_ref[...] + ch, g_ref[...], b_ref[...])
    gate = _sigmoid(_dot(y.astype(BF16), wg_bf[...]))
    o_ref[...] = y + gate * _dot(p_ref[...].astype(BF16), wp_bf[...])

    @pl.when(i == last)
    def _():
        drain(last, 1 - slot)


def _combine_ln_ple(res, ys, pos_flat, g1, g2, g, b, p, w_gate, w_proj, *, layer, tm):
    M, D = res.shape
    P = p.shape[2]
    tm = min(tm, M)
    tile = pl.BlockSpec((tm, D), lambda m, pos: (m, 0))
    col = pl.BlockSpec((tm, 1), lambda m, pos: (m, 0))
    vec = _resident((1, D), lambda m, pos: (0, 0))
    return pl.pallas_call(
        functools.partial(_combine_ln_ple_kernel, tm=tm, n_tokens=M),
        grid_spec=pltpu.PrefetchScalarGridSpec(
            num_scalar_prefetch=1,
            grid=(M // tm,),
            in_specs=[tile, col, col, vec, vec, pl.BlockSpec((None, tm, P), lambda m, pos: (layer, m, 0)),
                      _resident((None, D, D), lambda m, pos: (layer, 0, 0)),
                      _resident((None, P, D), lambda m, pos: (layer, 0, 0)),
                      pl.BlockSpec(memory_space=pl.ANY)],
            out_specs=tile,
            scratch_shapes=[pltpu.VMEM((2, TOP_K, tm, D), F32), pltpu.SemaphoreType.DMA((2,)),
                            pltpu.VMEM((D, D), BF16), pltpu.VMEM((P, D), BF16)]),
        out_shape=jax.ShapeDtypeStruct((M, D), F32),
        compiler_params=_params(1),
        name="moe_combine_ln_ple",
    )(pos_flat, res, g1, g2, g.reshape(1, D), b.reshape(1, D), p, w_gate, w_proj, ys)


def _moe_layer(xf, w_router, w_gu, w_down, ln_g, ln_b, p, w_gate, w_proj, *, ple_layer):
    T, D = xf.shape
    n_exp = w_router.shape[1]
    tm = min(MOE_TM, T)
    n_tiles = (T * TOP_K) // tm + n_exp

    route, counts = _router(xf, w_router, tm=512)
    idx = route[0:2].astype(jnp.int32)
    gates = route[2:4]
    rank = route[4:6].astype(jnp.int32)
    counts = counts[:n_exp, 0].astype(jnp.int32)
    tiles_per_expert = (counts + tm - 1) // tm
    tile_end = jnp.cumsum(tiles_per_expert)
    row_start = (tile_end - tiles_per_expert) * tm
    experts = jnp.arange(n_exp, dtype=jnp.int32)[:, None, None]
    pos = jnp.sum(jnp.where(idx[None] == experts, row_start[:, None, None], 0), axis=0) + rank
    pos_flat = pos.reshape(-1)
    pad_first = jnp.concatenate([row_start + counts, tile_end[-1:] * tm])
    pad_count = jnp.concatenate([tiles_per_expert * tm - counts, (n_tiles - tile_end[-1:]) * tm])
    pad_info = jnp.concatenate([pad_first, pad_count]).astype(jnp.int32)
    first_tile = (tile_end - tiles_per_expert).astype(jnp.int32)
    tile_count = tiles_per_expert.astype(jnp.int32)

    xs = _dispatch(xf, pos_flat, pad_info, n_rows=n_tiles * tm, tokens_per_step=512)
    hs = _grouped_mm(xs, w_gu, first_tile, tile_count, tm=tm, tn=896, tpi=2, gated=True, out_dtype=BF16)
    ys = _grouped_mm(hs, w_down, first_tile, tile_count, tm=tm, tn=512, tpi=1, gated=False, out_dtype=F32)
    return _combine_ln_ple(xf, ys, pos_flat, gates[0].reshape(T, 1), gates[1].reshape(T, 1), ln_g, ln_b,
                           p, w_gate, w_proj, layer=ple_layer, tm=256)


def kernel(x, p, fox_w_in, fox_b_f, fox_w_o, gm_w_in, gm_ln_v_g, gm_ln_v_b, gm_w_s, gm_b_s, gm_w_o, ffn_w_gu, ffn_w_down, moe_w_router, moe_w_gu, moe_w_down, ln_mix_g, ln_mix_b, ln_ch_g, ln_ch_b, ple_w_proj, ple_w_gate):
    B, S, D = x.shape
    T = B * S
    H = D // HEAD_DIM
    xf = x.reshape(T, D)
    pf = p.reshape(p.shape[0], T, p.shape[-1])

    w_in = fox_w_in[0]
    qkv = _mm(xf, w_in, n_cols=3 * D, tm=1024, tn=1024, out_dtype=BF16)
    c = _fox_gate(xf, w_in, fox_b_f[0], gate_col=3 * D, batch=B, ts=512)
    attn = _fox_attn(qkv, c, batch=B, tq=2048, n_sub=8)
    xf, xb = _mm_res_ln(attn, fox_w_o[0], xf, ln_mix_g[0], ln_mix_b[0], tm=256)
    hid = _swiglu_up(xb, ffn_w_gu[0], tm=1024, tn=512)
    xf, xb = _mm_res_ln(hid, ffn_w_down[0].astype(BF16), xf, ln_ch_g[0], ln_ch_b[0], tm=256)
    xf, xb = _ple(xb, xf, ple_w_gate, pf, ple_w_proj, layer=0, tm=512, tn=1024)

    z = _mm(xb, gm_w_in[0], n_cols=gm_w_in.shape[2], tm=1024, tn=1024, out_dtype=BF16, act="gelu")
    bias_full = jnp.repeat(gm_b_s[0].T.astype(F32), D // gm_b_s.shape[1], axis=1)
    y = _gm_spatial(z, gm_ln_v_g[0], gm_ln_v_b[0], gm_w_s[0], bias_full, tm=512)
    xf, xb = _mm_res_ln(y, gm_w_o[0], xf, ln_mix_g[1], ln_mix_b[1], tm=256)
    out = _moe_layer(xf, moe_w_router[0], moe_w_gu[0], moe_w_down[0], ln_ch_g[1], ln_ch_b[1],
                     pf, ple_w_gate, ple_w_proj, ple_layer=1)
    return out.reshape(B, S, D)
```

```python
import functools

import jax
import jax.numpy as jnp
from jax import lax
from jax.experimental import pallas as pl
from jax.experimental.pallas import tpu as pltpu

F32 = jnp.float32
BF16 = jnp.bfloat16

LN_EPS = 1e-5
DEPTH = 2
ALPHA = (2.0 * DEPTH) ** 0.25
HEAD_DIM = 128
GM_CHUNK = 128
GM_GROUPS = 16
TOP_K = 2
NEG = -1e30
V7X_VMEM_LIMIT_BYTES = 56 * 2**20
EXPERT_ROWS = 16
MOE_TM = 256
ROW_GROUP = 256


def _params(n_axes):
    return pltpu.CompilerParams(dimension_semantics=("arbitrary",) * n_axes,
                                vmem_limit_bytes=V7X_VMEM_LIMIT_BYTES)


def _dot(a, b):
    return jnp.dot(a, b, preferred_element_type=F32)


def _dot_nt(a, b):
    return lax.dot_general(a, b, (((1,), (1,)), ((), ())), preferred_element_type=F32)


def _sigmoid(x):
    return 1.0 / (1.0 + jnp.exp(-x))


def _gelu_tanh(x):
    return 0.5 * x * (1.0 + jnp.tanh(0.7978845608028654 * (x + 0.044715 * (x * x * x))))


def _layer_norm(y, g, b):
    mu = jnp.mean(y, axis=-1, keepdims=True)
    d = y - mu
    var = jnp.mean(d * d, axis=-1, keepdims=True)
    return d * lax.rsqrt(var + LN_EPS) * g + b


def _resident(block_shape, index_map):
    return pl.BlockSpec(block_shape, index_map, pipeline_mode=pl.Buffered(1))


def _bf16_weight(w_ref, cache, fresh):
    if w_ref.dtype == BF16:
        return w_ref[...]
    (cache_ref,) = cache

    @pl.when(fresh)
    def _():
        cache_ref[...] = w_ref[...].astype(BF16)

    return cache_ref[...]


def _weight_cache(w, block_shape):
    return [] if w.dtype == BF16 else [pltpu.VMEM(block_shape, BF16)]


def _row_groups(n_rows):
    group = min(ROW_GROUP, n_rows)
    return [slice(r, r + group) for r in range(0, n_rows, group)]


def _mm_kernel(a_ref, w_ref, o_ref, *cache, act):
    w = _bf16_weight(w_ref, cache, pl.program_id(1) == 0)
    for rows in _row_groups(a_ref.shape[0]):
        a = a_ref[rows, :]
        if a.dtype != BF16:
            a = a.astype(BF16)
        y = _dot(a, w)
        if act == "gelu":
            y = _gelu_tanh(y)
        o_ref[rows, :] = y.astype(o_ref.dtype)


def _mm(a, w, *, n_cols, tm, tn, out_dtype, act=None):
    M, K = a.shape
    tm, tn = min(tm, M), min(tn, n_cols)
    return pl.pallas_call(
        functools.partial(_mm_kernel, act=act),
        grid=(n_cols // tn, M // tm),
        in_specs=[pl.BlockSpec((tm, K), lambda n, m: (m, 0)),
                  pl.BlockSpec((K, tn), lambda n, m: (0, n))],
        out_specs=pl.BlockSpec((tm, tn), lambda n, m: (m, n)),
        out_shape=jax.ShapeDtypeStruct((M, n_cols), out_dtype),
        scratch_shapes=_weight_cache(w, (K, tn)),
        compiler_params=_params(2),
        name="mm_" + (act or "plain"),
    )(a, w)


def _mm_res_ln_kernel(a_ref, w_ref, r_ref, g_ref, b_ref, of_ref, ob_ref, *cache):
    w = _bf16_weight(w_ref, cache, pl.program_id(0) == 0)
    for rows in _row_groups(a_ref.shape[0]):
        y = ALPHA * r_ref[rows, :] + _dot(a_ref[rows, :], w)
        y = _layer_norm(y, g_ref[...], b_ref[...])
        of_ref[rows, :] = y
        ob_ref[rows, :] = y.astype(BF16)


def _mm_res_ln(a, w, res, g, b, *, tm):
    M, K = a.shape
    D = w.shape[1]
    tm = min(tm, M)
    return pl.pallas_call(
        _mm_res_ln_kernel,
        grid=(M // tm,),
        in_specs=[pl.BlockSpec((tm, K), lambda m: (m, 0)),
                  _resident((K, D), lambda m: (0, 0)),
                  pl.BlockSpec((tm, D), lambda m: (m, 0)),
                  _resident((1, D), lambda m: (0, 0)),
                  _resident((1, D), lambda m: (0, 0))],
        out_specs=[pl.BlockSpec((tm, D), lambda m: (m, 0)),
                   pl.BlockSpec((tm, D), lambda m: (m, 0))],
        out_shape=[jax.ShapeDtypeStruct((M, D), F32), jax.ShapeDtypeStruct((M, D), BF16)],
        scratch_shapes=_weight_cache(w, (K, D)),
        compiler_params=_params(1),
        name="mm_res_ln",
    )(a, w, res, g.reshape(1, D), b.reshape(1, D))


def _swiglu_up_kernel(x_ref, wg_ref, wu_ref, o_ref, *cache):
    fresh = pl.program_id(1) == 0
    wg = _bf16_weight(wg_ref, cache[:1], fresh)
    wu = _bf16_weight(wu_ref, cache[1:], fresh)
    for rows in _row_groups(x_ref.shape[0]):
        x = x_ref[rows, :]
        g = _dot(x, wg)
        u = _dot(x, wu)
        o_ref[rows, :] = (g * _sigmoid(g) * u).astype(o_ref.dtype)


def _swiglu_up(x, w_gu, *, tm, tn):
    M, K = x.shape
    F = w_gu.shape[1] // 2
    tm, tn = min(tm, M), min(tn, F)
    nf = F // tn
    return pl.pallas_call(
        _swiglu_up_kernel,
        grid=(nf, M // tm),
        in_specs=[pl.BlockSpec((tm, K), lambda n, m: (m, 0)),
                  pl.BlockSpec((K, tn), lambda n, m: (0, n)),
                  pl.BlockSpec((K, tn), lambda n, m: (0, nf + n))],
        out_specs=pl.BlockSpec((tm, tn), lambda n, m: (m, n)),
        out_shape=jax.ShapeDtypeStruct((M, F), BF16),
        scratch_shapes=_weight_cache(w_gu, (K, tn)) * 2,
        compiler_params=_params(2),
        name="swiglu_up",
    )(x, w_gu, w_gu)


def _ple_kernel(xb_ref, wg_ref, p_ref, wp_ref, xf_ref, of_ref, ob_ref, *cache):
    fresh = pl.program_id(1) == 0
    wg = _bf16_weight(wg_ref, cache[:1], fresh)
    wp = _bf16_weight(wp_ref, cache[1:], fresh)
    for rows in _row_groups(xb_ref.shape[0]):
        gate = _sigmoid(_dot(xb_ref[rows, :], wg))
        y = xf_ref[rows, :] + gate * _dot(p_ref[rows, :].astype(BF16), wp)
        of_ref[rows, :] = y
        ob_ref[rows, :] = y.astype(BF16)


def _ple(xb, xf, w_gate, p, w_proj, *, layer, tm, tn):
    M, D = xf.shape
    P = p.shape[2]
    tm, tn = min(tm, M), min(tn, D)
    return pl.pallas_call(
        _ple_kernel,
        grid=(D // tn, M // tm),
        in_specs=[pl.BlockSpec((tm, D), lambda n, m: (m, 0)),
                  pl.BlockSpec((None, D, tn), lambda n, m: (layer, 0, n)),
                  pl.BlockSpec((None, tm, P), lambda n, m: (layer, m, 0)),
                  pl.BlockSpec((None, P, tn), lambda n, m: (layer, 0, n)),
                  pl.BlockSpec((tm, tn), lambda n, m: (m, n))],
        out_specs=[pl.BlockSpec((tm, tn), lambda n, m: (m, n)),
                   pl.BlockSpec((tm, tn), lambda n, m: (m, n))],
        out_shape=[jax.ShapeDtypeStruct((M, D), F32), jax.ShapeDtypeStruct((M, D), BF16)],
        scratch_shapes=_weight_cache(w_gate, (D, tn)) + _weight_cache(w_proj, (P, tn)),
        compiler_params=_params(2),
        name="ple",
    )(xb, w_gate, p, w_proj, xf)


def _split3_bf16(x):
    hi = x.astype(BF16)
    r = x - hi.astype(F32)
    mid = r.astype(BF16)
    lo = (r - mid.astype(F32)).astype(BF16)
    return hi, mid, lo


def _fox_gate_kernel(x_ref, w_ref, bf_ref, c_ref, carry_ref, *, ts):
    @pl.when(pl.program_id(1) == 0)
    def _():
        carry_ref[...] = jnp.zeros_like(carry_ref)

    H = bf_ref.shape[0]
    f_all = _dot(x_ref[...].astype(BF16), w_ref[...].astype(BF16))
    f = jnp.transpose(f_all)[:H] + bf_ref[...]
    logf = jnp.minimum(f, 0.0) - jnp.log1p(jnp.exp(-jnp.abs(f)))
    row = lax.broadcasted_iota(jnp.int32, (ts, ts), 0)
    col = lax.broadcasted_iota(jnp.int32, (ts, ts), 1)
    tri = jnp.where(row <= col, 1.0, 0.0).astype(BF16)
    hi, mid, lo = _split3_bf16(logf)
    c = _dot(hi, tri) + _dot(mid, tri) + _dot(lo, tri) + carry_ref[...]
    c_ref[0] = c
    carry_ref[...] = c[:, ts - 1:ts]


def _fox_gate(x, w_in, b_f, *, gate_col, batch, ts):
    T, D = x.shape
    H = b_f.shape[0]
    S = T // batch
    ts = min(ts, S)
    ns = S // ts
    return pl.pallas_call(
        functools.partial(_fox_gate_kernel, ts=ts),
        grid=(batch, ns),
        in_specs=[pl.BlockSpec((ts, D), lambda b, i: (b * ns + i, 0)),
                  _resident((D, 128), lambda b, i: (0, gate_col // 128)),
                  _resident((H, 1), lambda b, i: (0, 0))],
        out_specs=pl.BlockSpec((1, H, ts), lambda b, i: (b, 0, i)),
        out_shape=jax.ShapeDtypeStruct((batch, H, S), F32),
        scratch_shapes=[pltpu.VMEM((H, 1), F32)],
        compiler_params=_params(2),
        name="fox_gate",
    )(x, w_in, b_f.reshape(H, 1).astype(F32))


def _fox_attn_kernel(q_ref, k_ref, v_ref, c_ref, o_ref, kaug, vaug, *, tq, n_sub, scale):
    qi = pl.program_id(2)
    sub = tq // n_sub
    hd = HEAD_DIM
    log2e = 1.4426950408889634
    S = k_ref.shape[0]

    @pl.when(qi == 0)
    def _():
        lane = lax.broadcasted_iota(jnp.int32, (S, hd), 1)
        c_col = jnp.transpose(jnp.broadcast_to(c_ref[0] * (-log2e), (8, S)))[:, 0:1]
        hi, mid, lo = _split3_bf16(c_col)
        extra = jnp.where(lane == 0, hi.astype(F32), jnp.where(lane == 1, mid.astype(F32),
                          jnp.where(lane == 2, lo.astype(F32), 0.0)))
        kaug[:, :hd] = k_ref[...]
        kaug[:, hd:] = extra.astype(BF16)
        vaug[:, :hd] = v_ref[...]
        vaug[:, hd:] = jnp.where(lane == 0, 1.0, 0.0).astype(BF16)

    q_lane = lax.broadcasted_iota(jnp.int32, (sub, hd), 1)
    q_extra = jnp.where(q_lane < 3, 1.0, 0.0).astype(BF16)
    qs = [jnp.concatenate([(q_ref[r * sub:(r + 1) * sub, :].astype(F32) * (scale * log2e)).astype(BF16), q_extra],
                          axis=1) for r in range(n_sub)]

    def update(carry, q, start, width, diagonal):
        m, acc = carry
        s = _dot_nt(q, kaug[pl.ds(start, width), :])
        if diagonal:
            row = lax.broadcasted_iota(jnp.int32, (sub, sub), 0)
            col = lax.broadcasted_iota(jnp.int32, (sub, sub), 1)
            last = jnp.where(row >= col, s[:, width - sub:], NEG)
            s = last if width == sub else jnp.concatenate([s[:, :width - sub], last], axis=1)
        m_new = jnp.maximum(m, jnp.max(s, axis=-1, keepdims=True))
        p = jnp.exp2(s - m_new).astype(BF16)
        acc = jnp.exp2(m - m_new) * acc + _dot(p, vaug[pl.ds(start, width), :])
        return m_new, acc

    def full_block(j, carries):
        start = pl.multiple_of(j * tq, tq)
        return tuple(update(carries[r], qs[r], start, tq, False) for r in range(n_sub))

    init = tuple((jnp.full((sub, 1), NEG, F32), jnp.zeros((sub, 2 * hd), F32)) for _ in range(n_sub))
    carries = lax.fori_loop(0, qi, full_block, init)

    start = pl.multiple_of(qi * tq, tq)
    for r in reversed(range(n_sub)):
        _, acc = update(carries[r], qs[r], start, (r + 1) * sub, True)
        o_ref[r * sub:(r + 1) * sub, :] = (acc[:, :hd] / acc[:, hd:hd + 1]).astype(o_ref.dtype)


def _fox_attn(qkv, c, *, batch, tq, n_sub):
    T, D3 = qkv.shape
    D = D3 // 3
    H = D // HEAD_DIM
    S = T // batch
    tq = min(tq, S)
    nq = S // tq
    c3 = c.reshape(batch * H, 1, S)
    return pl.pallas_call(
        functools.partial(_fox_attn_kernel, tq=tq, n_sub=n_sub, scale=HEAD_DIM ** -0.5),
        grid=(batch, H, nq),
        in_specs=[pl.BlockSpec((tq, HEAD_DIM), lambda b, h, i: (b * nq + i, h)),
                  pl.BlockSpec((S, HEAD_DIM), lambda b, h, i: (b, H + h)),
                  pl.BlockSpec((S, HEAD_DIM), lambda b, h, i: (b, 2 * H + h)),
                  pl.BlockSpec((1, 1, S), lambda b, h, i: (b * H + h, 0, 0))],
        out_specs=pl.BlockSpec((tq, HEAD_DIM), lambda b, h, i: (b * nq + i, h)),
        out_shape=jax.ShapeDtypeStruct((T, D), BF16),
        scratch_shapes=[pltpu.VMEM((S, 2 * HEAD_DIM), BF16), pltpu.VMEM((S, 2 * HEAD_DIM), BF16)],
        compiler_params=_params(3),
        name="fox_attn",
    )(qkv, qkv, qkv, c3)


def _gm_spatial_kernel(u_ref, v_ref, g_ref, b_ref, ws_ref, bs_ref, o_ref, *, n_chunks):
    C = GM_CHUNK
    vn = _layer_norm(v_ref[...].astype(F32), g_ref[...], b_ref[...]).astype(BF16)
    row = lax.broadcasted_iota(jnp.int32, (C, C), 0)
    col = lax.broadcasted_iota(jnp.int32, (C, C), 1)
    for grp in range(ws_ref.shape[0]):
        w = jnp.where(row >= col, ws_ref[grp], 0.0).astype(BF16)
        cols = slice(grp * C, (grp + 1) * C)
        for ch in range(n_chunks):
            rows = slice(ch * C, (ch + 1) * C)
            mixed = _dot(w, vn[rows, cols]) + bs_ref[:, cols]
            o_ref[rows, cols] = (u_ref[rows, cols].astype(F32) * mixed).astype(o_ref.dtype)


def _gm_spatial(z, ln_g, ln_b, w_s, bias_full, *, tm):
    T, W2 = z.shape
    W = W2 // 2
    G, C, _ = w_s.shape
    tm = min(tm, T)
    return pl.pallas_call(
        functools.partial(_gm_spatial_kernel, n_chunks=tm // C),
        grid=(T // tm,),
        in_specs=[pl.BlockSpec((tm, W), lambda i: (i, 0)),
                  pl.BlockSpec((tm, W), lambda i: (i, 1)),
                  _resident((1, W), lambda i: (0, 0)),
                  _resident((1, W), lambda i: (0, 0)),
                  _resident((G, C, C), lambda i: (0, 0, 0)),
                  _resident((C, W), lambda i: (0, 0))],
        out_specs=pl.BlockSpec((tm, W), lambda i: (i, 0)),
        out_shape=jax.ShapeDtypeStruct((T, W), BF16),
        compiler_params=_params(1),
        name="gm_spatial",
    )(z, z, ln_g.reshape(1, W), ln_b.reshape(1, W), w_s, bias_full)


def _router_kernel(x_ref, wrt_ref, o_ref, cnt_ref, carry_ref, *, tm, n_exp):
    @pl.when(pl.program_id(0) == 0)
    def _():
        carry_ref[...] = jnp.zeros_like(carry_ref)

    x = x_ref[...]
    xh = x.astype(BF16)
    xl = (x - xh.astype(F32)).astype(BF16)
    wh, wl = wrt_ref[0], wrt_ref[1]
    logits = _dot_nt(wh, xh) + _dot_nt(wh, xl) + _dot_nt(wl, xh)
    row = lax.broadcasted_iota(jnp.int32, logits.shape, 0)
    logits = jnp.where(row < n_exp, logits, NEG)
    top1 = jnp.max(logits, axis=0, keepdims=True)
    idx1 = jnp.min(jnp.where(logits == top1, row, EXPERT_ROWS), axis=0, keepdims=True)
    rest = jnp.where(row == idx1, NEG, logits)
    top2 = jnp.max(rest, axis=0, keepdims=True)
    idx2 = jnp.min(jnp.where(rest == top2, row, EXPERT_ROWS), axis=0, keepdims=True)
    e = jnp.exp(top2 - top1)
    gate1 = 1.0 / (1.0 + e)
    gate2 = e / (1.0 + e)

    sel = jnp.where((row == idx1) | (row == idx2), 1.0, 0.0)
    r = lax.broadcasted_iota(jnp.int32, (tm, tm), 0)
    c = lax.broadcasted_iota(jnp.int32, (tm, tm), 1)
    tri = jnp.where(r <= c, 1.0, 0.0).astype(BF16)
    incl = _dot(sel.astype(BF16), tri)
    excl = incl - sel + carry_ref[...]
    rank1 = jnp.sum(jnp.where(row == idx1, excl, 0.0), axis=0, keepdims=True)
    rank2 = jnp.sum(jnp.where(row == idx2, excl, 0.0), axis=0, keepdims=True)
    total = carry_ref[...] + incl[:, tm - 1:tm]
    carry_ref[...] = total
    cnt_ref[...] = jnp.broadcast_to(total, cnt_ref.shape)
    o_ref[...] = jnp.concatenate(
        [idx1.astype(F32), idx2.astype(F32), gate1, gate2, rank1, rank2,
         jnp.zeros((2, tm), F32)], axis=0)


def _router(xf, w_router, *, tm):
    T, D = xf.shape
    n_exp = w_router.shape[1]
    tm = min(tm, T)
    wt = jnp.zeros((EXPERT_ROWS, D), F32).at[:n_exp].set(w_router.T.astype(F32))
    wh = wt.astype(BF16)
    wl = (wt - wh.astype(F32)).astype(BF16)
    return pl.pallas_call(
        functools.partial(_router_kernel, tm=tm, n_exp=n_exp),
        grid=(T // tm,),
        in_specs=[pl.BlockSpec((tm, D), lambda i: (i, 0)),
                  _resident((2, EXPERT_ROWS, D), lambda i: (0, 0, 0))],
        out_specs=[pl.BlockSpec((8, tm), lambda i: (0, i)),
                   pl.BlockSpec((EXPERT_ROWS, 128), lambda i: (0, 0))],
        out_shape=[jax.ShapeDtypeStruct((8, T), F32),
                   jax.ShapeDtypeStruct((EXPERT_ROWS, 128), F32)],
        scratch_shapes=[pltpu.VMEM((EXPERT_ROWS, 1), F32)],
        compiler_params=_params(1),
        name="router",
    )(xf, jnp.stack([wh, wl]))


def _grouped_kernel(first_ref, count_ref, w_hbm, x_hbm, o_hbm, wbuf, w_bf, xbuf, obuf, state, w_sem, in_sem, out_sem,
                    *, tm, tn, tpi, n_tiles, gated):
    n_w = 2 if gated else 1
    n, e = pl.program_id(0), pl.program_id(1)
    nb, n_exp = pl.num_programs(0), pl.num_programs(1)
    step = n * n_exp + e
    first, count = first_ref[e], count_ref[e]
    col = pl.multiple_of(n * tn, 128)
    big = tpi * tm
    n_big, odd = count // tpi, count % tpi
    n_items = n_big + odd

    def w_copy(s, j):
        cols = pl.ds(pl.multiple_of((j * nb + s // n_exp) * tn, 128), tn)
        return pltpu.make_async_copy(w_hbm.at[s % n_exp, :, cols], wbuf.at[s % 2, j], w_sem.at[s % 2, j])

    def in_copy(tile, n_rows, slot):
        rows = pl.ds(pl.multiple_of(tile * tm, tm), n_rows)
        return pltpu.make_async_copy(x_hbm.at[rows], xbuf.at[slot, pl.ds(0, n_rows)], in_sem.at[slot])

    def out_copy(tile, n_rows, slot):
        rows = pl.ds(pl.multiple_of(tile * tm, tm), n_rows)
        return pltpu.make_async_copy(obuf.at[slot, pl.ds(0, n_rows)], o_hbm.at[rows, pl.ds(col, tn)],
                                     out_sem.at[slot])

    def for_item_size(tiles, fn):
        @pl.when(tiles == tpi)
        def _():
            fn(big)

        if tpi > 1:
            @pl.when(tiles == 1)
            def _():
                fn(tm)

    def start_first_item_of(group, slot):
        tiles = jnp.minimum(count_ref[group], tpi)
        for_item_size(tiles, lambda n_rows: in_copy(first_ref[group], n_rows, slot).start())

    def start_next_steps_first_item(slot):
        @pl.when(step + 1 < nb * n_exp)
        def _():
            start_first_item_of((step + 1) % n_exp, slot)

    def retire_output(slot):
        for_item_size(state[1 + slot], lambda n_rows: out_copy(state[3 + slot], n_rows, slot).wait())
        state[1 + slot] = 0

    def run_item(tile, n_rows, slot):
        retire_output(slot)
        x = xbuf[slot, :n_rows]
        if x.dtype != BF16:
            x = x.astype(BF16)
        y = _dot(x, w_bf[0])
        if gated:
            y = y * _sigmoid(y) * _dot(x, w_bf[1])
        obuf[slot, :n_rows] = y.astype(obuf.dtype)
        out_copy(tile, n_rows, slot).start()
        state[1 + slot] = n_rows // tm
        state[3 + slot] = tile

    @pl.when(step == 0)
    def _():
        for i in range(5):
            state[i] = 0
        for j in range(n_w):
            w_copy(step, j).start(priority=1)
        start_first_item_of(e, 0)

    @pl.when(step + 1 < nb * n_exp)
    def _():
        for j in range(n_w):
            w_copy(step + 1, j).start(priority=1)

    for j in range(n_w):
        w_copy(step, j).wait()

    issued = state[0]

    @pl.when(count == 0)
    def _():
        start_next_steps_first_item(issued % 2)

    @pl.when(count > 0)
    def _():
        for j in range(n_w):
            w_bf[j] = wbuf[step % 2, j].astype(BF16)

        def body(k, c):
            slot = (issued + k) % 2
            tile = first + k * tpi
            in_copy(tile, big, slot).wait()

            @pl.when(k + 1 < n_big)
            def _():
                in_copy(tile + tpi, big, 1 - slot).start()

            if tpi > 1:
                @pl.when(jnp.logical_and(k + 1 == n_big, odd > 0))
                def _():
                    in_copy(tile + tpi, tm, 1 - slot).start()

            @pl.when(k + 1 == n_items)
            def _():
                start_next_steps_first_item(1 - slot)

            run_item(tile, big, slot)
            return c

        lax.fori_loop(0, n_big, body, 0)

        if tpi > 1:
            @pl.when(odd > 0)
            def _():
                slot = (issued + n_big) % 2
                tile = first + n_big * tpi
                in_copy(tile, tm, slot).wait()
                start_next_steps_first_item(1 - slot)
                run_item(tile, tm, slot)

        state[0] = issued + n_items

    @pl.when(e == n_exp - 1)
    def _():
        retire_output(0)
        retire_output(1)
        obuf[0] = jnp.zeros(obuf.shape[1:], obuf.dtype)

        def start_zero(t, c):
            out_copy(t, tm, 0).start()
            return c

        def wait_zero(t, c):
            out_copy(t, tm, 0).wait()
            return c

        lax.fori_loop(first + count, n_tiles, start_zero, 0)
        lax.fori_loop(first + count, n_tiles, wait_zero, 0)


def _grouped_mm(xs, w, first_tile, tile_count, *, tm, tn, tpi, gated, out_dtype):
    assert tpi in (1, 2)
    P, K = xs.shape
    E = w.shape[0]
    N = w.shape[2] // 2 if gated else w.shape[2]
    tn = min(tn, N)
    n_w = 2 if gated else 1
    return pl.pallas_call(
        functools.partial(_grouped_kernel, tm=tm, tn=tn, tpi=tpi, n_tiles=P // tm, gated=gated),
        grid_spec=pltpu.PrefetchScalarGridSpec(
            num_scalar_prefetch=2,
            grid=(N // tn, E),
            in_specs=[pl.BlockSpec(memory_space=pl.ANY), pl.BlockSpec(memory_space=pl.ANY)],
            out_specs=pl.BlockSpec(memory_space=pl.ANY),
            scratch_shapes=[pltpu.VMEM((2, n_w, K, tn), w.dtype), pltpu.VMEM((n_w, K, tn), BF16),
                            pltpu.VMEM((2, tpi * tm, K), xs.dtype), pltpu.VMEM((2, tpi * tm, tn), out_dtype),
                            pltpu.SMEM((5,), jnp.int32), pltpu.SemaphoreType.DMA((2, n_w)),
                            pltpu.SemaphoreType.DMA((2,)), pltpu.SemaphoreType.DMA((2,))]),
        out_shape=jax.ShapeDtypeStruct((P, N), out_dtype),
        compiler_params=_params(2),
        name="moe_up" if gated else "moe_down",
    )(first_tile, tile_count, w, xs)


def _dispatch_kernel(pos_ref, pad_ref, x_ref, xs_hbm, zero_ref, sem, *, tokens_per_step, n_tokens, n_groups):
    base = pl.program_id(0) * tokens_per_step

    def token_copy(j, k):
        dst = pos_ref[k * n_tokens + base + j]
        return pltpu.make_async_copy(x_ref.at[pl.ds(j, 1)], xs_hbm.at[pl.ds(dst, 1)], sem)

    def pad_copy(dst):
        return pltpu.make_async_copy(zero_ref, xs_hbm.at[pl.ds(dst, 1)], sem)

    @pl.when(pl.program_id(0) == 0)
    def _():
        zero_ref[...] = jnp.zeros_like(zero_ref)
        for e in range(n_groups):
            first, count = pad_ref[e], pad_ref[n_groups + e]

            def start_pad(j, c):
                pad_copy(first + j).start()
                return c

            def wait_pad(j, c):
                pad_copy(first + j).wait()
                return c

            lax.fori_loop(0, count, start_pad, 0)
            lax.fori_loop(0, count, wait_pad, 0)

    def start_rows(j, c):
        for k in range(TOP_K):
            token_copy(j, k).start()
        return c

    def wait_rows(j, c):
        for k in range(TOP_K):
            token_copy(j, k).wait()
        return c

    lax.fori_loop(0, tokens_per_step, start_rows, 0, unroll=16)
    lax.fori_loop(0, tokens_per_step, wait_rows, 0, unroll=16)


def _dispatch(xf, pos_flat, pad_info, *, n_rows, tokens_per_step):
    T, D = xf.shape
    tokens_per_step = min(tokens_per_step, T)
    return pl.pallas_call(
        functools.partial(_dispatch_kernel, tokens_per_step=tokens_per_step, n_tokens=T,
                          n_groups=pad_info.shape[0] // 2),
        grid_spec=pltpu.PrefetchScalarGridSpec(
            num_scalar_prefetch=2,
            grid=(T // tokens_per_step,),
            in_specs=[pl.BlockSpec((tokens_per_step, D), lambda i, pos, pad: (i, 0))],
            out_specs=pl.BlockSpec(memory_space=pl.ANY),
            scratch_shapes=[pltpu.VMEM((1, D), F32), pltpu.SemaphoreType.DMA(())]),
        out_shape=jax.ShapeDtypeStruct((n_rows, D), F32),
        compiler_params=_params(1),
        name="moe_dispatch",
    )(pos_flat, pad_info, xf)


def _combine_ln_ple_kernel(pos_ref, r_ref, g1_ref, g2_ref, g_ref, b_ref, p_ref, wg_ref, wp_ref, ys_hbm, o_ref,
                           buf, sem, wg_bf, wp_bf, *, tm, n_tokens):
    i = pl.program_id(0)

    def row_copy(tile, slot, j, k):
        src = pos_ref[k * n_tokens + tile * tm + j]
        return pltpu.make_async_copy(ys_hbm.at[pl.ds(src, 1)], buf.at[slot, k, pl.ds(j, 1)], sem.at[slot])

    def fetch(tile, slot):
        for j in range(tm):
            for k in range(TOP_K):
                row_copy(tile, slot, j, k).start()

    def drain(tile, slot):
        for j in range(tm):
            for k in range(TOP_K):
                row_copy(tile, slot, j, k).wait()

    @pl.when(i == 0)
    def _():
        fetch(0, 0)
        wg_bf[...] = wg_ref[...].astype(BF16)
        wp_bf[...] = wp_ref[...].astype(BF16)

    last = pl.num_programs(0) - 1
    slot = i % 2
    fetch(jnp.minimum(i + 1, last), 1 - slot)
    drain(i, slot)
    ch = g1_ref[...] * buf[slot, 0] + g2_ref[...] * buf[slot, 1]
    y = _layer_norm(ALPHA * r_ref[...] + ch, g_ref[...], b_ref[...])
    gate = _sigmoid(_dot(y.astype(BF16), wg_bf[...]))
    o_ref[...] = y + gate * _dot(p_ref[...].astype(BF16), wp_bf[...])

    @pl.when(i == last)
    def _():
        drain(last, 1 - slot)


def _combine_ln_ple(res, ys, pos_flat, g1, g2, g, b, p, w_gate, w_proj, *, layer, tm):
    M, D = res.shape
    P = p.shape[2]
    tm = min(tm, M)
    tile = pl.BlockSpec((tm, D), lambda m, pos: (m, 0))
    col = pl.BlockSpec((tm, 1), lambda m, pos: (m, 0))
    vec = _resident((1, D), lambda m, pos: (0, 0))
    return pl.pallas_call(
        functools.partial(_combine_ln_ple_kernel, tm=tm, n_tokens=M),
        grid_spec=pltpu.PrefetchScalarGridSpec(
            num_scalar_prefetch=1,
            grid=(M // tm,),
            in_specs=[tile, col, col, vec, vec, pl.BlockSpec((None, tm, P), lambda m, pos: (layer, m, 0)),
                      _resident((None, D, D), lambda m, pos: (layer, 0, 0)),
                      _resident((None, P, D), lambda m, pos: (layer, 0, 0)),
                      pl.BlockSpec(memory_space=pl.ANY)],
            out_specs=tile,
            scratch_shapes=[pltpu.VMEM((2, TOP_K, tm, D), F32), pltpu.SemaphoreType.DMA((2,)),
                            pltpu.VMEM((D, D), BF16), pltpu.VMEM((P, D), BF16)]),
        out_shape=jax.ShapeDtypeStruct((M, D), F32),
        compiler_params=_params(1),
        name="moe_combine_ln_ple",
    )(pos_flat, res, g1, g2, g.reshape(1, D), b.reshape(1, D), p, w_gate, w_proj, ys)


def _moe_layer(xf, w_router, w_gu, w_down, ln_g, ln_b, p, w_gate, w_proj, *, ple_layer):
    T, D = xf.shape
    n_exp = w_router.shape[1]
    tm = min(MOE_TM, T)
    n_tiles = (T * TOP_K) // tm + n_exp

    route, counts = _router(xf, w_router, tm=512)
    idx = route[0:2].astype(jnp.int32)
    gates = route[2:4]
    rank = route[4:6].astype(jnp.int32)
    counts = counts[:n_exp, 0].astype(jnp.int32)
    tiles_per_expert = (counts + tm - 1) // tm
    tile_end = jnp.cumsum(tiles_per_expert)
    row_start = (tile_end - tiles_per_expert) * tm
    experts = jnp.arange(n_exp, dtype=jnp.int32)[:, None, None]
    pos = jnp.sum(jnp.where(idx[None] == experts, row_start[:, None, None], 0), axis=0) + rank
    pos_flat = pos.reshape(-1)
    pad_first = jnp.concatenate([row_start + counts, tile_end[-1:] * tm])
    pad_count = jnp.concatenate([tiles_per_expert * tm - counts, (n_tiles - tile_end[-1:]) * tm])
    pad_info = jnp.concatenate([pad_first, pad_count]).astype(jnp.int32)
    first_tile = (tile_end - tiles_per_expert).astype(jnp.int32)
    tile_count = tiles_per_expert.astype(jnp.int32)

    xs = _dispatch(xf, pos_flat, pad_info, n_rows=n_tiles * tm, tokens_per_step=512)
    hs = _grouped_mm(xs, w_gu, first_tile, tile_count, tm=tm, tn=896, tpi=2, gated=True, out_dtype=BF16)
    ys = _grouped_mm(hs, w_down, first_tile, tile_count, tm=tm, tn=512, tpi=1, gated=False, out_dtype=F32)
    return _combine_ln_ple(xf, ys, pos_flat, gates[0].reshape(T, 1), gates[1].reshape(T, 1), ln_g, ln_b,
                           p, w_gate, w_proj, layer=ple_layer, tm=256)


def kernel(x, p, fox_w_in, fox_b_f, fox_w_o, gm_w_in, gm_ln_v_g, gm_ln_v_b, gm_w_s, gm_b_s, gm_w_o, ffn_w_gu, ffn_w_down, moe_w_router, moe_w_gu, moe_w_down, ln_mix_g, ln_mix_b, ln_ch_g, ln_ch_b, ple_w_proj, ple_w_gate):
    B, S, D = x.shape
    T = B * S
    H = D // HEAD_DIM
    xf = x.reshape(T, D)
    pf = p.reshape(p.shape[0], T, p.shape[-1])

    w_in = fox_w_in[0]
    qkv = _mm(xf, w_in, n_cols=3 * D, tm=1024, tn=1024, out_dtype=BF16)
    c = _fox_gate(xf, w_in, fox_b_f[0], gate_col=3 * D, batch=B, ts=512)
    attn = _fox_attn(qkv, c, batch=B, tq=4096, n_sub=16)
    xf, xb = _mm_res_ln(attn, fox_w_o[0], xf, ln_mix_g[0], ln_mix_b[0], tm=256)
    hid = _swiglu_up(xb, ffn_w_gu[0], tm=1024, tn=512)
    xf, xb = _mm_res_ln(hid, ffn_w_down[0].astype(BF16), xf, ln_ch_g[0], ln_ch_b[0], tm=256)
    xf, xb = _ple(xb, xf, ple_w_gate, pf, ple_w_proj, layer=0, tm=512, tn=1024)

    z = _mm(xb, gm_w_in[0], n_cols=gm_w_in.shape[2], tm=1024, tn=1024, out_dtype=BF16, act="gelu")
    bias_full = jnp.repeat(gm_b_s[0].T.astype(F32), D // gm_b_s.shape[1], axis=1)
    y = _gm_spatial(z, gm_ln_v_g[0], gm_ln_v_b[0], gm_w_s[0], bias_full, tm=512)
    xf, xb = _mm_res_ln(y, gm_w_o[0], xf, ln_mix_g[1], ln_mix_b[1], tm=256)
    out = _moe_layer(xf, moe_w_router[0], moe_w_gu[0], moe_w_down[0], ln_ch_g[1], ln_ch_b[1],
                     pf, ple_w_gate, ple_w_proj, ple_layer=1)
    return out.reshape(B, S, D)
```

```python
import functools

import jax
import jax.numpy as jnp
from jax import lax
from jax.experimental import pallas as pl
from jax.experimental.pallas import tpu as pltpu

F32 = jnp.float32
BF16 = jnp.bfloat16

LN_EPS = 1e-5
DEPTH = 2
ALPHA = (2.0 * DEPTH) ** 0.25
HEAD_DIM = 128
GM_CHUNK = 128
GM_GROUPS = 16
TOP_K = 2
NEG = -1e30
V7X_VMEM_LIMIT_BYTES = 56 * 2**20
EXPERT_ROWS = 16
MOE_TM = 256
ROW_GROUP = 256


def _params(n_axes):
    return pltpu.CompilerParams(dimension_semantics=("arbitrary",) * n_axes,
                                vmem_limit_bytes=V7X_VMEM_LIMIT_BYTES)


def _dot(a, b):
    return jnp.dot(a, b, preferred_element_type=F32)


def _dot_nt(a, b):
    return lax.dot_general(a, b, (((1,), (1,)), ((), ())), preferred_element_type=F32)


def _sigmoid(x):
    return 1.0 / (1.0 + jnp.exp(-x))


def _gelu_tanh(x):
    return 0.5 * x * (1.0 + jnp.tanh(0.7978845608028654 * (x + 0.044715 * (x * x * x))))


def _layer_norm(y, g, b):
    mu = jnp.mean(y, axis=-1, keepdims=True)
    d = y - mu
    var = jnp.mean(d * d, axis=-1, keepdims=True)
    return d * lax.rsqrt(var + LN_EPS) * g + b


def _resident(block_shape, index_map):
    return pl.BlockSpec(block_shape, index_map, pipeline_mode=pl.Buffered(1))


def _bf16_weight(w_ref, cache, fresh):
    if w_ref.dtype == BF16:
        return w_ref[...]
    (cache_ref,) = cache

    @pl.when(fresh)
    def _():
        cache_ref[...] = w_ref[...].astype(BF16)

    return cache_ref[...]


def _weight_cache(w, block_shape):
    return [] if w.dtype == BF16 else [pltpu.VMEM(block_shape, BF16)]


def _row_groups(n_rows):
    group = min(ROW_GROUP, n_rows)
    return [slice(r, r + group) for r in range(0, n_rows, group)]


def _mm_kernel(a_ref, w_ref, o_ref, *cache, act):
    w = _bf16_weight(w_ref, cache, pl.program_id(1) == 0)
    for rows in _row_groups(a_ref.shape[0]):
        a = a_ref[rows, :]
        if a.dtype != BF16:
            a = a.astype(BF16)
        y = _dot(a, w)
        if act == "gelu":
            y = _gelu_tanh(y)
        o_ref[rows, :] = y.astype(o_ref.dtype)


def _mm(a, w, *, n_cols, tm, tn, out_dtype, act=None):
    M, K = a.shape
    tm, tn = min(tm, M), min(tn, n_cols)
    return pl.pallas_call(
        functools.partial(_mm_kernel, act=act),
        grid=(n_cols // tn, M // tm),
        in_specs=[pl.BlockSpec((tm, K), lambda n, m: (m, 0)),
                  pl.BlockSpec((K, tn), lambda n, m: (0, n))],
        out_specs=pl.BlockSpec((tm, tn), lambda n, m: (m, n)),
        out_shape=jax.ShapeDtypeStruct((M, n_cols), out_dtype),
        scratch_shapes=_weight_cache(w, (K, tn)),
        compiler_params=_params(2),
        name="mm_" + (act or "plain"),
    )(a, w)


def _mm_res_ln_kernel(a_ref, w_ref, r_ref, g_ref, b_ref, of_ref, ob_ref, *cache):
    w = _bf16_weight(w_ref, cache, pl.program_id(0) == 0)
    for rows in _row_groups(a_ref.shape[0]):
        y = ALPHA * r_ref[rows, :] + _dot(a_ref[rows, :], w)
        y = _layer_norm(y, g_ref[...], b_ref[...])
        of_ref[rows, :] = y
        ob_ref[rows, :] = y.astype(BF16)


def _mm_res_ln(a, w, res, g, b, *, tm):
    M, K = a.shape
    D = w.shape[1]
    tm = min(tm, M)
    return pl.pallas_call(
        _mm_res_ln_kernel,
        grid=(M // tm,),
        in_specs=[pl.BlockSpec((tm, K), lambda m: (m, 0)),
                  _resident((K, D), lambda m: (0, 0)),
                  pl.BlockSpec((tm, D), lambda m: (m, 0)),
                  _resident((1, D), lambda m: (0, 0)),
                  _resident((1, D), lambda m: (0, 0))],
        out_specs=[pl.BlockSpec((tm, D), lambda m: (m, 0)),
                   pl.BlockSpec((tm, D), lambda m: (m, 0))],
        out_shape=[jax.ShapeDtypeStruct((M, D), F32), jax.ShapeDtypeStruct((M, D), BF16)],
        scratch_shapes=_weight_cache(w, (K, D)),
        compiler_params=_params(1),
        name="mm_res_ln",
    )(a, w, res, g.reshape(1, D), b.reshape(1, D))


def _swiglu_up_kernel(x_ref, wg_ref, wu_ref, o_ref, *cache):
    fresh = pl.program_id(1) == 0
    wg = _bf16_weight(wg_ref, cache[:1], fresh)
    wu = _bf16_weight(wu_ref, cache[1:], fresh)
    for rows in _row_groups(x_ref.shape[0]):
        x = x_ref[rows, :]
        g = _dot(x, wg)
        u = _dot(x, wu)
        o_ref[rows, :] = (g * _sigmoid(g) * u).astype(o_ref.dtype)


def _swiglu_up(x, w_gu, *, tm, tn):
    M, K = x.shape
    F = w_gu.shape[1] // 2
    tm, tn = min(tm, M), min(tn, F)
    nf = F // tn
    return pl.pallas_call(
        _swiglu_up_kernel,
        grid=(nf, M // tm),
        in_specs=[pl.BlockSpec((tm, K), lambda n, m: (m, 0)),
                  pl.BlockSpec((K, tn), lambda n, m: (0, n)),
                  pl.BlockSpec((K, tn), lambda n, m: (0, nf + n))],
        out_specs=pl.BlockSpec((tm, tn), lambda n, m: (m, n)),
        out_shape=jax.ShapeDtypeStruct((M, F), BF16),
        scratch_shapes=_weight_cache(w_gu, (K, tn)) * 2,
        compiler_params=_params(2),
        name="swiglu_up",
    )(x, w_gu, w_gu)


def _ple_kernel(xb_ref, wg_ref, p_ref, wp_ref, xf_ref, of_ref, ob_ref, *cache):
    fresh = pl.program_id(1) == 0
    wg = _bf16_weight(wg_ref, cache[:1], fresh)
    wp = _bf16_weight(wp_ref, cache[1:], fresh)
    for rows in _row_groups(xb_ref.shape[0]):
        gate = _sigmoid(_dot(xb_ref[rows, :], wg))
        y = xf_ref[rows, :] + gate * _dot(p_ref[rows, :].astype(BF16), wp)
        of_ref[rows, :] = y
        ob_ref[rows, :] = y.astype(BF16)


def _ple(xb, xf, w_gate, p, w_proj, *, layer, tm, tn):
    M, D = xf.shape
    P = p.shape[2]
    tm, tn = min(tm, M), min(tn, D)
    return pl.pallas_call(
        _ple_kernel,
        grid=(D // tn, M // tm),
        in_specs=[pl.BlockSpec((tm, D), lambda n, m: (m, 0)),
                  pl.BlockSpec((None, D, tn), lambda n, m: (layer, 0, n)),
                  pl.BlockSpec((None, tm, P), lambda n, m: (layer, m, 0)),
                  pl.BlockSpec((None, P, tn), lambda n, m: (layer, 0, n)),
                  pl.BlockSpec((tm, tn), lambda n, m: (m, n))],
        out_specs=[pl.BlockSpec((tm, tn), lambda n, m: (m, n)),
                   pl.BlockSpec((tm, tn), lambda n, m: (m, n))],
        out_shape=[jax.ShapeDtypeStruct((M, D), F32), jax.ShapeDtypeStruct((M, D), BF16)],
        scratch_shapes=_weight_cache(w_gate, (D, tn)) + _weight_cache(w_proj, (P, tn)),
        compiler_params=_params(2),
        name="ple",
    )(xb, w_gate, p, w_proj, xf)


def _split3_bf16(x):
    hi = x.astype(BF16)
    r = x - hi.astype(F32)
    mid = r.astype(BF16)
    lo = (r - mid.astype(F32)).astype(BF16)
    return hi, mid, lo


def _fox_gate_kernel(x_ref, w_ref, bf_ref, c_ref, carry_ref, *, ts):
    @pl.when(pl.program_id(1) == 0)
    def _():
        carry_ref[...] = jnp.zeros_like(carry_ref)

    H = bf_ref.shape[0]
    f_all = _dot(x_ref[...].astype(BF16), w_ref[...].astype(BF16))
    f = jnp.transpose(f_all)[:H] + bf_ref[...]
    logf = jnp.minimum(f, 0.0) - jnp.log1p(jnp.exp(-jnp.abs(f)))
    row = lax.broadcasted_iota(jnp.int32, (ts, ts), 0)
    col = lax.broadcasted_iota(jnp.int32, (ts, ts), 1)
    tri = jnp.where(row <= col, 1.0, 0.0).astype(BF16)
    hi, mid, lo = _split3_bf16(logf)
    c = _dot(hi, tri) + _dot(mid, tri) + _dot(lo, tri) + carry_ref[...]
    c_ref[0] = c
    carry_ref[...] = c[:, ts - 1:ts]


def _fox_gate(x, w_in, b_f, *, gate_col, batch, ts):
    T, D = x.shape
    H = b_f.shape[0]
    S = T // batch
    ts = min(ts, S)
    ns = S // ts
    return pl.pallas_call(
        functools.partial(_fox_gate_kernel, ts=ts),
        grid=(batch, ns),
        in_specs=[pl.BlockSpec((ts, D), lambda b, i: (b * ns + i, 0)),
                  _resident((D, 128), lambda b, i: (0, gate_col // 128)),
                  _resident((H, 1), lambda b, i: (0, 0))],
        out_specs=pl.BlockSpec((1, H, ts), lambda b, i: (b, 0, i)),
        out_shape=jax.ShapeDtypeStruct((batch, H, S), F32),
        scratch_shapes=[pltpu.VMEM((H, 1), F32)],
        compiler_params=_params(2),
        name="fox_gate",
    )(x, w_in, b_f.reshape(H, 1).astype(F32))


def _fox_attn_kernel(q_ref, k_ref, v_ref, c_ref, o_ref, kaug, vaug, *, tq, n_sub, scale):
    qi = pl.program_id(2)
    sub = tq // n_sub
    hd = HEAD_DIM
    log2e = 1.4426950408889634
    S = k_ref.shape[0]

    @pl.when(qi == 0)
    def _():
        lane = lax.broadcasted_iota(jnp.int32, (S, hd), 1)
        c_col = jnp.transpose(jnp.broadcast_to(c_ref[0] * (-log2e), (8, S)))[:, 0:1]
        hi, mid, lo = _split3_bf16(c_col)
        extra = jnp.where(lane == 0, hi.astype(F32), jnp.where(lane == 1, mid.astype(F32),
                          jnp.where(lane == 2, lo.astype(F32), 0.0)))
        kaug[:, :hd] = k_ref[...]
        kaug[:, hd:] = extra.astype(BF16)
        vaug[:, :hd] = v_ref[...]
        vaug[:, hd:] = jnp.where(lane == 0, 1.0, 0.0).astype(BF16)

    q_lane = lax.broadcasted_iota(jnp.int32, (sub, hd), 1)
    q_extra = jnp.where(q_lane < 3, 1.0, 0.0).astype(BF16)
    qs = [jnp.concatenate([(q_ref[r * sub:(r + 1) * sub, :].astype(F32) * (scale * log2e)).astype(BF16), q_extra],
                          axis=1) for r in range(n_sub)]

    def update(carry, q, start, width, diagonal):
        m, acc = carry
        s = _dot_nt(q, kaug[pl.ds(start, width), :])
        if diagonal:
            row = lax.broadcasted_iota(jnp.int32, (sub, sub), 0)
            col = lax.broadcasted_iota(jnp.int32, (sub, sub), 1)
            last = jnp.where(row >= col, s[:, width - sub:], NEG)
            s = last if width == sub else jnp.concatenate([s[:, :width - sub], last], axis=1)
        m_new = jnp.maximum(m, jnp.max(s, axis=-1, keepdims=True))
        p = jnp.exp2(s - m_new).astype(BF16)
        acc = jnp.exp2(m - m_new) * acc + _dot(p, vaug[pl.ds(start, width), :])
        return m_new, acc

    def full_block(j, carries):
        start = pl.multiple_of(j * tq, tq)
        return tuple(update(carries[r], qs[r], start, tq, False) for r in range(n_sub))

    init = tuple((jnp.full((sub, 1), NEG, F32), jnp.zeros((sub, 2 * hd), F32)) for _ in range(n_sub))
    carries = lax.fori_loop(0, qi, full_block, init)

    start = pl.multiple_of(qi * tq, tq)
    for r in reversed(range(n_sub)):
        _, acc = update(carries[r], qs[r], start, (r + 1) * sub, True)
        o_ref[r * sub:(r + 1) * sub, :] = (acc[:, :hd] / acc[:, hd:hd + 1]).astype(o_ref.dtype)


def _fox_attn(qkv, c, *, batch, tq, n_sub):
    T, D3 = qkv.shape
    D = D3 // 3
    H = D // HEAD_DIM
    S = T // batch
    tq = min(tq, S)
    nq = S // tq
    c3 = c.reshape(batch * H, 1, S)
    return pl.pallas_call(
        functools.partial(_fox_attn_kernel, tq=tq, n_sub=n_sub, scale=HEAD_DIM ** -0.5),
        grid=(batch, H, nq),
        in_specs=[pl.BlockSpec((tq, HEAD_DIM), lambda b, h, i: (b * nq + i, h)),
                  pl.BlockSpec((S, HEAD_DIM), lambda b, h, i: (b, H + h)),
                  pl.BlockSpec((S, HEAD_DIM), lambda b, h, i: (b, 2 * H + h)),
                  pl.BlockSpec((1, 1, S), lambda b, h, i: (b * H + h, 0, 0))],
        out_specs=pl.BlockSpec((tq, HEAD_DIM), lambda b, h, i: (b * nq + i, h)),
        out_shape=jax.ShapeDtypeStruct((T, D), BF16),
        scratch_shapes=[pltpu.VMEM((S, 2 * HEAD_DIM), BF16), pltpu.VMEM((S, 2 * HEAD_DIM), BF16)],
        compiler_params=_params(3),
        name="fox_attn",
    )(qkv, qkv, qkv, c3)


def _gm_spatial_kernel(u_ref, v_ref, g_ref, b_ref, ws_ref, bs_ref, o_ref, *, n_chunks):
    C = GM_CHUNK
    vn = _layer_norm(v_ref[...].astype(F32), g_ref[...], b_ref[...]).astype(BF16)
    row = lax.broadcasted_iota(jnp.int32, (C, C), 0)
    col = lax.broadcasted_iota(jnp.int32, (C, C), 1)
    for grp in range(ws_ref.shape[0]):
        w = jnp.where(row >= col, ws_ref[grp], 0.0).astype(BF16)
        cols = slice(grp * C, (grp + 1) * C)
        for ch in range(n_chunks):
            rows = slice(ch * C, (ch + 1) * C)
            mixed = _dot(w, vn[rows, cols]) + bs_ref[:, cols]
            o_ref[rows, cols] = (u_ref[rows, cols].astype(F32) * mixed).astype(o_ref.dtype)


def _gm_spatial(z, ln_g, ln_b, w_s, bias_full, *, tm):
    T, W2 = z.shape
    W = W2 // 2
    G, C, _ = w_s.shape
    tm = min(tm, T)
    return pl.pallas_call(
        functools.partial(_gm_spatial_kernel, n_chunks=tm // C),
        grid=(T // tm,),
        in_specs=[pl.BlockSpec((tm, W), lambda i: (i, 0)),
                  pl.BlockSpec((tm, W), lambda i: (i, 1)),
                  _resident((1, W), lambda i: (0, 0)),
                  _resident((1, W), lambda i: (0, 0)),
                  _resident((G, C, C), lambda i: (0, 0, 0)),
                  _resident((C, W), lambda i: (0, 0))],
        out_specs=pl.BlockSpec((tm, W), lambda i: (i, 0)),
        out_shape=jax.ShapeDtypeStruct((T, W), BF16),
        compiler_params=_params(1),
        name="gm_spatial",
    )(z, z, ln_g.reshape(1, W), ln_b.reshape(1, W), w_s, bias_full)


def _router_kernel(x_ref, wrt_ref, o_ref, cnt_ref, carry_ref, *, tm, n_exp):
    @pl.when(pl.program_id(0) == 0)
    def _():
        carry_ref[...] = jnp.zeros_like(carry_ref)

    x = x_ref[...]
    xh = x.astype(BF16)
    xl = (x - xh.astype(F32)).astype(BF16)
    wh, wl = wrt_ref[0], wrt_ref[1]
    logits = _dot_nt(wh, xh) + _dot_nt(wh, xl) + _dot_nt(wl, xh)
    row = lax.broadcasted_iota(jnp.int32, logits.shape, 0)
    logits = jnp.where(row < n_exp, logits, NEG)
    top1 = jnp.max(logits, axis=0, keepdims=True)
    idx1 = jnp.min(jnp.where(logits == top1, row, EXPERT_ROWS), axis=0, keepdims=True)
    rest = jnp.where(row == idx1, NEG, logits)
    top2 = jnp.max(rest, axis=0, keepdims=True)
    idx2 = jnp.min(jnp.where(rest == top2, row, EXPERT_ROWS), axis=0, keepdims=True)
    e = jnp.exp(top2 - top1)
    gate1 = 1.0 / (1.0 + e)
    gate2 = e / (1.0 + e)

    sel = jnp.where((row == idx1) | (row == idx2), 1.0, 0.0)
    r = lax.broadcasted_iota(jnp.int32, (tm, tm), 0)
    c = lax.broadcasted_iota(jnp.int32, (tm, tm), 1)
    tri = jnp.where(r <= c, 1.0, 0.0).astype(BF16)
    incl = _dot(sel.astype(BF16), tri)
    excl = incl - sel + carry_ref[...]
    rank1 = jnp.sum(jnp.where(row == idx1, excl, 0.0), axis=0, keepdims=True)
    rank2 = jnp.sum(jnp.where(row == idx2, excl, 0.0), axis=0, keepdims=True)
    total = carry_ref[...] + incl[:, tm - 1:tm]
    carry_ref[...] = total
    cnt_ref[...] = jnp.broadcast_to(total, cnt_ref.shape)
    o_ref[...] = jnp.concatenate(
        [idx1.astype(F32), idx2.astype(F32), gate1, gate2, rank1, rank2,
         jnp.zeros((2, tm), F32)], axis=0)


def _router(xf, w_router, *, tm):
    T, D = xf.shape
    n_exp = w_router.shape[1]
    tm = min(tm, T)
    wt = jnp.zeros((EXPERT_ROWS, D), F32).at[:n_exp].set(w_router.T.astype(F32))
    wh = wt.astype(BF16)
    wl = (wt - wh.astype(F32)).astype(BF16)
    return pl.pallas_call(
        functools.partial(_router_kernel, tm=tm, n_exp=n_exp),
        grid=(T // tm,),
        in_specs=[pl.BlockSpec((tm, D), lambda i: (i, 0)),
                  _resident((2, EXPERT_ROWS, D), lambda i: (0, 0, 0))],
        out_specs=[pl.BlockSpec((8, tm), lambda i: (0, i)),
                   pl.BlockSpec((EXPERT_ROWS, 128), lambda i: (0, 0))],
        out_shape=[jax.ShapeDtypeStruct((8, T), F32),
                   jax.ShapeDtypeStruct((EXPERT_ROWS, 128), F32)],
        scratch_shapes=[pltpu.VMEM((EXPERT_ROWS, 1), F32)],
        compiler_params=_params(1),
        name="router",
    )(xf, jnp.stack([wh, wl]))


def _grouped_kernel(first_ref, count_ref, w_hbm, x_hbm, o_hbm, wbuf, w_bf, xbuf, obuf, state, w_sem, in_sem, out_sem,
                    *, tm, tn, tpi, n_tiles, gated):
    n_w = 2 if gated else 1
    n, e = pl.program_id(0), pl.program_id(1)
    nb, n_exp = pl.num_programs(0), pl.num_programs(1)
    step = n * n_exp + e
    first, count = first_ref[e], count_ref[e]
    col = pl.multiple_of(n * tn, 128)
    big = tpi * tm
    n_big, odd = count // tpi, count % tpi
    n_items = n_big + odd

    def w_copy(s, j):
        cols = pl.ds(pl.multiple_of((j * nb + s // n_exp) * tn, 128), tn)
        return pltpu.make_async_copy(w_hbm.at[s % n_exp, :, cols], wbuf.at[s % 2, j], w_sem.at[s % 2, j])

    def in_copy(tile, n_rows, slot):
        rows = pl.ds(pl.multiple_of(tile * tm, tm), n_rows)
        return pltpu.make_async_copy(x_hbm.at[rows], xbuf.at[slot, pl.ds(0, n_rows)], in_sem.at[slot])

    def out_copy(tile, n_rows, slot):
        rows = pl.ds(pl.multiple_of(tile * tm, tm), n_rows)
        return pltpu.make_async_copy(obuf.at[slot, pl.ds(0, n_rows)], o_hbm.at[rows, pl.ds(col, tn)],
                                     out_sem.at[slot])

    def for_item_size(tiles, fn):
        @pl.when(tiles == tpi)
        def _():
            fn(big)

        if tpi > 1:
            @pl.when(tiles == 1)
            def _():
                fn(tm)

    def start_first_item_of(group, slot):
        tiles = jnp.minimum(count_ref[group], tpi)
        for_item_size(tiles, lambda n_rows: in_copy(first_ref[group], n_rows, slot).start())

    def start_next_steps_first_item(slot):
        @pl.when(step + 1 < nb * n_exp)
        def _():
            start_first_item_of((step + 1) % n_exp, slot)

    def retire_output(slot):
        for_item_size(state[1 + slot], lambda n_rows: out_copy(state[3 + slot], n_rows, slot).wait())
        state[1 + slot] = 0

    def run_item(tile, n_rows, slot):
        retire_output(slot)
        x = xbuf[slot, :n_rows]
        if x.dtype != BF16:
            x = x.astype(BF16)
        y = _dot(x, w_bf[0])
        if gated:
            y = y * _sigmoid(y) * _dot(x, w_bf[1])
        obuf[slot, :n_rows] = y.astype(obuf.dtype)
        out_copy(tile, n_rows, slot).start()
        state[1 + slot] = n_rows // tm
        state[3 + slot] = tile

    @pl.when(step == 0)
    def _():
        for i in range(5):
            state[i] = 0
        for j in range(n_w):
            w_copy(step, j).start(priority=1)
        start_first_item_of(e, 0)

    @pl.when(step + 1 < nb * n_exp)
    def _():
        for j in range(n_w):
            w_copy(step + 1, j).start(priority=1)

    for j in range(n_w):
        w_copy(step, j).wait()

    issued = state[0]

    @pl.when(count == 0)
    def _():
        start_next_steps_first_item(issued % 2)

    @pl.when(count > 0)
    def _():
        for j in range(n_w):
            w_bf[j] = wbuf[step % 2, j].astype(BF16)

        def body(k, c):
            slot = (issued + k) % 2
            tile = first + k * tpi
            in_copy(tile, big, slot).wait()

            @pl.when(k + 1 < n_big)
            def _():
                in_copy(tile + tpi, big, 1 - slot).start()

            if tpi > 1:
                @pl.when(jnp.logical_and(k + 1 == n_big, odd > 0))
                def _():
                    in_copy(tile + tpi, tm, 1 - slot).start()

            @pl.when(k + 1 == n_items)
            def _():
                start_next_steps_first_item(1 - slot)

            run_item(tile, big, slot)
            return c

        lax.fori_loop(0, n_big, body, 0)

        if tpi > 1:
            @pl.when(odd > 0)
            def _():
                slot = (issued + n_big) % 2
                tile = first + n_big * tpi
                in_copy(tile, tm, slot).wait()
                start_next_steps_first_item(1 - slot)
                run_item(tile, tm, slot)

        state[0] = issued + n_items

    @pl.when(e == n_exp - 1)
    def _():
        retire_output(0)
        retire_output(1)
        obuf[0] = jnp.zeros(obuf.shape[1:], obuf.dtype)

        def start_zero(t, c):
            out_copy(t, tm, 0).start()
            return c

        def wait_zero(t, c):
            out_copy(t, tm, 0).wait()
            return c

        lax.fori_loop(first + count, n_tiles, start_zero, 0)
        lax.fori_loop(first + count, n_tiles, wait_zero, 0)


def _grouped_mm(xs, w, first_tile, tile_count, *, tm, tn, tpi, gated, out_dtype):
    assert tpi in (1, 2)
    P, K = xs.shape
    E = w.shape[0]
    N = w.shape[2] // 2 if gated else w.shape[2]
    tn = min(tn, N)
    n_w = 2 if gated else 1
    return pl.pallas_call(
        functools.partial(_grouped_kernel, tm=tm, tn=tn, tpi=tpi, n_tiles=P // tm, gated=gated),
        grid_spec=pltpu.PrefetchScalarGridSpec(
            num_scalar_prefetch=2,
            grid=(N // tn, E),
            in_specs=[pl.BlockSpec(memory_space=pl.ANY), pl.BlockSpec(memory_space=pl.ANY)],
            out_specs=pl.BlockSpec(memory_space=pl.ANY),
            scratch_shapes=[pltpu.VMEM((2, n_w, K, tn), w.dtype), pltpu.VMEM((n_w, K, tn), BF16),
                            pltpu.VMEM((2, tpi * tm, K), xs.dtype), pltpu.VMEM((2, tpi * tm, tn), out_dtype),
                            pltpu.SMEM((5,), jnp.int32), pltpu.SemaphoreType.DMA((2, n_w)),
                            pltpu.SemaphoreType.DMA((2,)), pltpu.SemaphoreType.DMA((2,))]),
        out_shape=jax.ShapeDtypeStruct((P, N), out_dtype),
        compiler_params=_params(2),
        name="moe_up" if gated else "moe_down",
    )(first_tile, tile_count, w, xs)


def _dispatch_kernel(pos_ref, pad_ref, x_ref, xs_hbm, zero_ref, sem, *, tokens_per_step, n_tokens, n_groups):
    base = pl.program_id(0) * tokens_per_step

    def token_copy(j, k):
        dst = pos_ref[k * n_tokens + base + j]
        return pltpu.make_async_copy(x_ref.at[pl.ds(j, 1)], xs_hbm.at[pl.ds(dst, 1)], sem)

    def pad_copy(dst):
        return pltpu.make_async_copy(zero_ref, xs_hbm.at[pl.ds(dst, 1)], sem)

    @pl.when(pl.program_id(0) == 0)
    def _():
        zero_ref[...] = jnp.zeros_like(zero_ref)
        for e in range(n_groups):
            first, count = pad_ref[e], pad_ref[n_groups + e]

            def start_pad(j, c):
                pad_copy(first + j).start()
                return c

            def wait_pad(j, c):
                pad_copy(first + j).wait()
                return c

            lax.fori_loop(0, count, start_pad, 0)
            lax.fori_loop(0, count, wait_pad, 0)

    def start_rows(j, c):
        for k in range(TOP_K):
            token_copy(j, k).start()
        return c

    def wait_rows(j, c):
        for k in range(TOP_K):
            token_copy(j, k).wait()
        return c

    lax.fori_loop(0, tokens_per_step, start_rows, 0, unroll=16)
    lax.fori_loop(0, tokens_per_step, wait_rows, 0, unroll=16)


def _dispatch(xf, pos_flat, pad_info, *, n_rows, tokens_per_step):
    T, D = xf.shape
    tokens_per_step = min(tokens_per_step, T)
    return pl.pallas_call(
        functools.partial(_dispatch_kernel, tokens_per_step=tokens_per_step, n_tokens=T,
                          n_groups=pad_info.shape[0] // 2),
        grid_spec=pltpu.PrefetchScalarGridSpec(
            num_scalar_prefetch=2,
            grid=(T // tokens_per_step,),
            in_specs=[pl.BlockSpec((tokens_per_step, D), lambda i, pos, pad: (i, 0))],
            out_specs=pl.BlockSpec(memory_space=pl.ANY),
            scratch_shapes=[pltpu.VMEM((1, D), F32), pltpu.SemaphoreType.DMA(())]),
        out_shape=jax.ShapeDtypeStruct((n_rows, D), F32),
        compiler_params=_params(1),
        name="moe_dispatch",
    )(pos_flat, pad_info, xf)


def _combine_ln_ple_kernel(pos_ref, r_ref, g1_ref, g2_ref, g_ref, b_ref, p_ref, wg_ref, wp_ref, ys_hbm, o_ref,
                           buf, sem, wg_bf, wp_bf, *, tm, n_tokens):
    i = pl.program_id(0)

    def row_copy(tile, slot, j, k):
        src = pos_ref[k * n_tokens + tile * tm + j]
        return pltpu.make_async_copy(ys_hbm.at[pl.ds(src, 1)], buf.at[slot, k, pl.ds(j, 1)], sem.at[slot])

    def fetch(tile, slot):
        for j in range(tm):
            for k in range(TOP_K):
                row_copy(tile, slot, j, k).start()

    def drain(tile, slot):
        for j in range(tm):
            for k in range(TOP_K):
                row_copy(tile, slot, j, k).wait()

    @pl.when(i == 0)
    def _():
        fetch(0, 0)
        wg_bf[...] = wg_ref[...].astype(BF16)
        wp_bf[...] = wp_ref[...].astype(BF16)

    last = pl.num_programs(0) - 1
    slot = i % 2
    fetch(jnp.minimum(i + 1, last), 1 - slot)
    drain(i, slot)
    ch = g1_ref[...] * buf[slot, 0] + g2_ref[...] * buf[slot, 1]
    y = _layer_norm(ALPHA * r_ref[...] + ch, g_ref[...], b_ref[...])
    gate = _sigmoid(_dot(y.astype(BF16), wg_bf[...]))
    o_ref[...] = y + gate * _dot(p_ref[...].astype(BF16), wp_bf[...])

    @pl.when(i == last)
    def _():
        drain(last, 1 - slot)


def _combine_ln_ple(res, ys, pos_flat, g1, g2, g, b, p, w_gate, w_proj, *, layer, tm):
    M, D = res.shape
    P = p.shape[2]
    tm = min(tm, M)
    tile = pl.BlockSpec((tm, D), lambda m, pos: (m, 0))
    col = pl.BlockSpec((tm, 1), lambda m, pos: (m, 0))
    vec = _resident((1, D), lambda m, pos: (0, 0))
    return pl.pallas_call(
        functools.partial(_combine_ln_ple_kernel, tm=tm, n_tokens=M),
        grid_spec=pltpu.PrefetchScalarGridSpec(
            num_scalar_prefetch=1,
            grid=(M // tm,),
            in_specs=[tile, col, col, vec, vec, pl.BlockSpec((None, tm, P), lambda m, pos: (layer, m, 0)),
                      _resident((None, D, D), lambda m, pos: (layer, 0, 0)),
                      _resident((None, P, D), lambda m, pos: (layer, 0, 0)),
                      pl.BlockSpec(memory_space=pl.ANY)],
            out_specs=tile,
            scratch_shapes=[pltpu.VMEM((2, TOP_K, tm, D), F32), pltpu.SemaphoreType.DMA((2,)),
                            pltpu.VMEM((D, D), BF16), pltpu.VMEM((P, D), BF16)]),
        out_shape=jax.ShapeDtypeStruct((M, D), F32),
        compiler_params=_params(1),
        name="moe_combine_ln_ple",
    )(pos_flat, res, g1, g2, g.reshape(1, D), b.reshape(1, D), p, w_gate, w_proj, ys)


def _moe_layer(xf, w_router, w_gu, w_down, ln_g, ln_b, p, w_gate, w_proj, *, ple_layer):
    T, D = xf.shape
    n_exp = w_router.shape[1]
    tm = min(MOE_TM, T)
    n_tiles = (T * TOP_K) // tm + n_exp

    route, counts = _router(xf, w_router, tm=512)
    idx = route[0:2].astype(jnp.int32)
    gates = route[2:4]
    rank = route[4:6].astype(jnp.int32)
    counts = counts[:n_exp, 0].astype(jnp.int32)
    tiles_per_expert = (counts + tm - 1) // tm
    tile_end = jnp.cumsum(tiles_per_expert)
    row_start = (tile_end - tiles_per_expert) * tm
    experts = jnp.arange(n_exp, dtype=jnp.int32)[:, None, None]
    pos = jnp.sum(jnp.where(idx[None] == experts, row_start[:, None, None], 0), axis=0) + rank
    pos_flat = pos.reshape(-1)
    pad_first = jnp.concatenate([row_start + counts, tile_end[-1:] * tm])
    pad_count = jnp.concatenate([tiles_per_expert * tm - counts, (n_tiles - tile_end[-1:]) * tm])
    pad_info = jnp.concatenate([pad_first, pad_count]).astype(jnp.int32)
    first_tile = (tile_end - tiles_per_expert).astype(jnp.int32)
    tile_count = tiles_per_expert.astype(jnp.int32)

    xs = _dispatch(xf, pos_flat, pad_info, n_rows=n_tiles * tm, tokens_per_step=512)
    hs = _grouped_mm(xs, w_gu, first_tile, tile_count, tm=tm, tn=896, tpi=2, gated=True, out_dtype=BF16)
    ys = _grouped_mm(hs, w_down, first_tile, tile_count, tm=tm, tn=512, tpi=1, gated=False, out_dtype=F32)
    return _combine_ln_ple(xf, ys, pos_flat, gates[0].reshape(T, 1), gates[1].reshape(T, 1), ln_g, ln_b,
                           p, w_gate, w_proj, layer=ple_layer, tm=256)


def kernel(x, p, fox_w_in, fox_b_f, fox_w_o, gm_w_in, gm_ln_v_g, gm_ln_v_b, gm_w_s, gm_b_s, gm_w_o, ffn_w_gu, ffn_w_down, moe_w_router, moe_w_gu, moe_w_down, ln_mix_g, ln_mix_b, ln_ch_g, ln_ch_b, ple_w_proj, ple_w_gate):
    B, S, D = x.shape
    T = B * S
    H = D // HEAD_DIM
    xf = x.reshape(T, D)
    pf = p.reshape(p.shape[0], T, p.shape[-1])

    w_in = fox_w_in[0]
    qkv = _mm(xf, w_in, n_cols=3 * D, tm=1024, tn=1024, out_dtype=BF16)
    c = _fox_gate(xf, w_in, fox_b_f[0], gate_col=3 * D, batch=B, ts=512)
    attn = _fox_attn(qkv, c, batch=B, tq=4096, n_sub=8)
    xf, xb = _mm_res_ln(attn, fox_w_o[0], xf, ln_mix_g[0], ln_mix_b[0], tm=256)
    hid = _swiglu_up(xb, ffn_w_gu[0], tm=1024, tn=512)
    xf, xb = _mm_res_ln(hid, ffn_w_down[0].astype(BF16), xf, ln_ch_g[0], ln_ch_b[0], tm=256)
    xf, xb = _ple(xb, xf, ple_w_gate, pf, ple_w_proj, layer=0, tm=512, tn=1024)

    z = _mm(xb, gm_w_in[0], n_cols=gm_w_in.shape[2], tm=1024, tn=1024, out_dtype=BF16, act="gelu")
    bias_full = jnp.repeat(gm_b_s[0].T.astype(F32), D // gm_b_s.shape[1], axis=1)
    y = _gm_spatial(z, gm_ln_v_g[0], gm_ln_v_b[0], gm_w_s[0], bias_full, tm=512)
    xf, xb = _mm_res_ln(y, gm_w_o[0], xf, ln_mix_g[1], ln_mix_b[1], tm=256)
    out = _moe_layer(xf, moe_w_router[0], moe_w_gu[0], moe_w_down[0], ln_ch_g[1], ln_ch_b[1],
                     pf, ple_w_gate, ple_w_proj, ple_layer=1)
    return out.reshape(B, S, D)
```
